```python
import jax, jax.numpy as jnp
from jax import lax
import numpy as np

D_MODEL = 1024
BATCH = 8
SEQ = 4096
DEPTH = 2

HEAD_DIM = 64
SB_HEADS = 8
NSA_HEADS = 8
NSA_KV_HEADS = 2
NSA_GROUP = NSA_HEADS // NSA_KV_HEADS
SB_WIDTH = SB_HEADS * HEAD_DIM
NSA_WIDTH = NSA_HEADS * HEAD_DIM
KV_WIDTH = NSA_KV_HEADS * HEAD_DIM
N_BRANCH = 3
QKV_WIDTH = 3 * SB_WIDTH + NSA_WIDTH + 2 * N_BRANCH * KV_WIDTH + N_BRANCH * NSA_HEADS
CMP_LEN = 32
CMP_STRIDE = 16
CMP_HIDDEN = 256
SEL_LEN = 64
SEL_TOPK = 16
N_LOCAL = 2
WINDOW = 512
Q_BLOCK = 128
ROPE_THETA = 10000.0
D_FF = 2816
CONV_W = 3
EPS = 1e-6
NEG = -1e30
FORCED_SCORE = 1e6

kernel_name = 'hybrid_sbattn_nsa_convffn_adaln'


def rms_norm(x, g):
    xf = x.astype(jnp.float32)
    y = xf * lax.rsqrt(jnp.mean(xf * xf, axis=-1, keepdims=True) + EPS)
    return (y * g.astype(jnp.float32)).astype(x.dtype)


def rope(x, pos):
    half = HEAD_DIM // 2
    inv = ROPE_THETA ** (-jnp.arange(half, dtype=jnp.float32) / half)
    ang = pos.astype(jnp.float32)[:, None] * inv[None, :]
    cos = jnp.cos(ang)[:, None, :]
    sin = jnp.sin(ang)[:, None, :]
    xf = x.astype(jnp.float32)
    x1, x2 = xf[..., :half], xf[..., half:]
    return jnp.concatenate([x1 * cos - x2 * sin, x2 * cos + x1 * sin], axis=-1).astype(x.dtype)


def masked_softmax(s, mask):
    s = jnp.where(mask, s, NEG)
    m = jnp.max(s, axis=-1, keepdims=True)
    p = jnp.exp(s - m) * mask
    return p / jnp.maximum(jnp.sum(p, axis=-1, keepdims=True), 1e-6)


def stick_breaking_attention(q, k, v):
    B, T, H, D = q.shape
    nb = T // Q_BLOCK
    scale = D ** -0.5
    qb = q.reshape(B, nb, Q_BLOCK, H, D).transpose(1, 0, 3, 2, 4)
    key_pos = jnp.arange(T)

    def block(args):
        i, qi = args
        z = jnp.einsum('bhqd,bshd->bhqs', qi, k).astype(jnp.float32) * scale
        q_pos = i * Q_BLOCK + jnp.arange(Q_BLOCK)
        mask = key_pos[None, :] < q_pos[:, None]
        log_1m = jnp.where(mask, jax.nn.log_sigmoid(-z), 0.0)
        tail = lax.cumsum(log_1m, axis=3, reverse=True) - log_1m
        w = jnp.where(mask, jnp.exp(jax.nn.log_sigmoid(z) + tail), 0.0)
        return jnp.einsum('bhqs,bshd->bqhd', w, v.astype(jnp.float32))

    out = lax.map(block, (jnp.arange(nb), qb))
    return out.transpose(1, 0, 2, 3, 4).reshape(B, T, H, D)


def compress_blocks(x, pos_emb, w1, w2):
    B, T, G, D = x.shape
    r = CMP_LEN // CMP_STRIDE
    n_chunks = T // CMP_STRIDE
    n = n_chunks - r + 1
    chunks = x.reshape(B, n_chunks, CMP_STRIDE, G, D)
    blocks = jnp.concatenate([chunks[:, j:j + n] for j in range(r)], axis=2)
    blocks = blocks + pos_emb[None, None, :, None, :]
    flat = blocks.transpose(0, 1, 3, 2, 4).reshape(B, n, G, CMP_LEN * D)
    return jax.nn.gelu(flat @ w1) @ w2


def nsa_attention(q, k_c, v_c, k_s, v_s, k_w, v_w, gates,
                  cmp_pos_k, cmp_w1_k, cmp_w2_k, cmp_pos_v, cmp_w1_v, cmp_w2_v):
    B, T, H, D = q.shape
    G, HG = NSA_KV_HEADS, NSA_GROUP
    scale = D ** -0.5
    kc = compress_blocks(k_c, cmp_pos_k, cmp_w1_k, cmp_w2_k)
    vc = compress_blocks(v_c, cmp_pos_v, cmp_w1_v, cmp_w2_v).astype(jnp.float32)
    n_cmp = kc.shape[1]
    cmp_start = jnp.arange(n_cmp) * CMP_STRIDE
    cmp_end = cmp_start + CMP_LEN - 1
    kc = rope(kc, cmp_end)
    n_sel = T // SEL_LEN
    sel_start = jnp.arange(n_sel) * SEL_LEN
    ov = jnp.clip(jnp.minimum(cmp_start[:, None] + CMP_LEN, sel_start[None, :] + SEL_LEN)
                  - jnp.maximum(cmp_start[:, None], sel_start[None, :]), 0, None)
    ov = ov.astype(jnp.float32) / CMP_LEN
    ks_blocks = k_s.reshape(B, n_sel, SEL_LEN, G, D).transpose(0, 3, 1, 2, 4)
    vs_blocks = v_s.reshape(B, n_sel, SEL_LEN, G, D).transpose(0, 3, 1, 2, 4)
    top = min(SEL_TOPK, n_sel)
    kw_pad = jnp.pad(k_w, ((0, 0), (WINDOW, 0), (0, 0), (0, 0)))
    vw_pad = jnp.pad(v_w, ((0, 0), (WINDOW, 0), (0, 0), (0, 0)))
    nb = T // Q_BLOCK
    qb = q.reshape(B, nb, Q_BLOCK, G, HG, D).transpose(1, 0, 2, 3, 4, 5)
    gb = gates.reshape(B, nb, Q_BLOCK, G, HG, N_BRANCH).transpose(1, 0, 2, 3, 4, 5)
    b_ix = jnp.arange(B)[:, None, None, None]
    g_ix = jnp.arange(G)[None, :, None, None]
    sel_ids = jnp.arange(n_sel)

    def block(args):
        i, qi, gi = args
        q_pos = i * Q_BLOCK + jnp.arange(Q_BLOCK)
        s_c = jnp.einsum('bqghd,bngd->bghqn', qi, kc).astype(jnp.float32) * scale
        p_c = masked_softmax(s_c, cmp_end[None, :] <= q_pos[:, None])
        o_c = jnp.einsum('bghqn,bngd->bqghd', p_c, vc)
        imp = jnp.einsum('bghqn,nj->bgqj', p_c, ov)
        dist = (q_pos // SEL_LEN)[:, None] - sel_ids[None, :]
        forced = (sel_ids[None, :] == 0) | ((dist >= 0) & (dist < N_LOCAL))
        score = jnp.where(dist < 0, -1.0, jnp.where(forced, FORCED_SCORE, imp))
        top_score, idx = lax.top_k(score, top)
        kg = ks_blocks[b_ix, g_ix, idx].reshape(B, G, Q_BLOCK, top * SEL_LEN, D)
        vg = vs_blocks[b_ix, g_ix, idx].reshape(B, G, Q_BLOCK, top * SEL_LEN, D)
        tok = (idx[..., None] * SEL_LEN + jnp.arange(SEL_LEN)).reshape(B, G, Q_BLOCK, top * SEL_LEN)
        smask = (tok <= q_pos[None, None, :, None]) & jnp.repeat(top_score >= 0, SEL_LEN, axis=-1)
        s_s = jnp.einsum('bqghd,bgqsd->bghqs', qi, kg).astype(jnp.float32) * scale
        p_s = masked_softmax(s_s, smask[:, :, None])
        o_s = jnp.einsum('bghqs,bgqsd->bqghd', p_s, vg.astype(jnp.float32))
        kw = lax.dynamic_slice_in_dim(kw_pad, i * Q_BLOCK, WINDOW + Q_BLOCK, axis=1)
        vw = lax.dynamic_slice_in_dim(vw_pad, i * Q_BLOCK, WINDOW + Q_BLOCK, axis=1)
        kpos = i * Q_BLOCK - WINDOW + jnp.arange(WINDOW + Q_BLOCK)
        rel = q_pos[:, None] - kpos[None, :]
        wmask = (rel >= 0) & (rel < WINDOW) & (kpos[None, :] >= 0)
        s_w = jnp.einsum('bqghd,bsgd->bghqs', qi, kw).astype(jnp.float32) * scale
        p_w = masked_softmax(s_w, wmask)
        o_w = jnp.einsum('bghqs,bsgd->bqghd', p_w, vw.astype(jnp.float32))
        gf = gi.astype(jnp.float32)
        return gf[..., 0:1] * o_c + gf[..., 1:2] * o_s + gf[..., 2:3] * o_w

    out = lax.map(block, (jnp.arange(nb), qb, gb))
    return out.transpose(1, 0, 2, 3, 4, 5).reshape(B, T, H * D)


def causal_depthwise_conv(u, w, b):
    C = u.shape[-1]
    y = lax.conv_general_dilated(u, w[:, None, :].astype(u.dtype), window_strides=(1,),
                                 padding=[(CONV_W - 1, 0)],
                                 dimension_numbers=('NWC', 'WIO', 'NWC'),
                                 feature_group_count=C)
    return y + b


def token_mixer(h, w_in, cmp_pos_k, cmp_w1_k, cmp_w2_k, cmp_pos_v, cmp_w1_v, cmp_w2_v,
                sb_out_g, nsa_out_g, w_out):
    B, T, _ = h.shape
    proj = h @ w_in
    sizes = [SB_WIDTH] * 3 + [NSA_WIDTH] + [KV_WIDTH] * (2 * N_BRANCH)
    offsets = [int(v) for v in np.cumsum(sizes)]
    parts = jnp.split(proj, offsets, axis=-1)
    sb_q, sb_k, sb_v, nq, kc, vc, ks, vs, kw, vw, gl = parts
    hd = lambda t, n: t.reshape(B, T, n, HEAD_DIM)
    pos = jnp.arange(T)
    o_sb = stick_breaking_attention(hd(sb_q, SB_HEADS), hd(sb_k, SB_HEADS), hd(sb_v, SB_HEADS))
    o_sb = rms_norm(o_sb.reshape(B, T, SB_WIDTH), sb_out_g)
    gates = jax.nn.sigmoid(gl.astype(jnp.float32)).reshape(B, T, NSA_HEADS, N_BRANCH)
    o_nsa = nsa_attention(rope(hd(nq, NSA_HEADS), pos),
                          hd(kc, NSA_KV_HEADS), hd(vc, NSA_KV_HEADS),
                          rope(hd(ks, NSA_KV_HEADS), pos), hd(vs, NSA_KV_HEADS),
                          rope(hd(kw, NSA_KV_HEADS), pos), hd(vw, NSA_KV_HEADS),
                          gates, cmp_pos_k, cmp_w1_k, cmp_w2_k, cmp_pos_v, cmp_w1_v, cmp_w2_v)
    o_nsa = rms_norm(o_nsa, nsa_out_g)
    mixed = jnp.concatenate([o_sb.astype(h.dtype), o_nsa.astype(h.dtype)], axis=-1)
    return mixed @ w_out


def channel_mixer(h, ffn_w_in, ffn_conv_w, ffn_conv_b, ffn_w_down):
    u = causal_depthwise_conv(h @ ffn_w_in, ffn_conv_w, ffn_conv_b)
    a, b = jnp.split(u, 2, axis=-1)
    return (jax.nn.silu(a) * b) @ ffn_w_down


def setup_inputs(seed: int = 0) -> dict:
    key = jax.random.key(seed)
    ks = jax.random.split(key, 24)
    f32 = jnp.float32
    nrm = lambda k, shape, s: jax.random.normal(k, shape, f32) * s
    L = DEPTH
    return {
        'x': nrm(ks[0], (BATCH, SEQ, D_MODEL), 1.0),
        'c': nrm(ks[1], (BATCH, D_MODEL), 1.0),
        'ln1_g': 1.0 + nrm(ks[2], (L, D_MODEL), 0.02),
        'ln2_g': 1.0 + nrm(ks[3], (L, D_MODEL), 0.02),
        'w_ada': nrm(ks[4], (L, D_MODEL, 6 * D_MODEL), 0.5 * D_MODEL ** -0.5),
        'b_ada': nrm(ks[5], (L, 6 * D_MODEL), 0.02),
        'w_in': nrm(ks[6], (L, D_MODEL, QKV_WIDTH), D_MODEL ** -0.5),
        'cmp_pos_k': nrm(ks[7], (L, CMP_LEN, HEAD_DIM), 0.1),
        'cmp_w1_k': nrm(ks[8], (L, CMP_LEN * HEAD_DIM, CMP_HIDDEN), (CMP_LEN * HEAD_DIM) ** -0.5),
        'cmp_w2_k': nrm(ks[9], (L, CMP_HIDDEN, HEAD_DIM), CMP_HIDDEN ** -0.5),
        'cmp_pos_v': nrm(ks[10], (L, CMP_LEN, HEAD_DIM), 0.1),
        'cmp_w1_v': nrm(ks[11], (L, CMP_LEN * HEAD_DIM, CMP_HIDDEN), (CMP_LEN * HEAD_DIM) ** -0.5),
        'cmp_w2_v': nrm(ks[12], (L, CMP_HIDDEN, HEAD_DIM), CMP_HIDDEN ** -0.5),
        'sb_out_g': 1.0 + nrm(ks[13], (L, SB_WIDTH), 0.02),
        'nsa_out_g': 1.0 + nrm(ks[14], (L, NSA_WIDTH), 0.02),
        'w_out': nrm(ks[15], (L, D_MODEL, D_MODEL), D_MODEL ** -0.5),
        'ffn_w_in': nrm(ks[16], (L, D_MODEL, 2 * D_FF), D_MODEL ** -0.5),
        'ffn_conv_w': nrm(ks[17], (L, CONV_W, 2 * D_FF), CONV_W ** -0.5),
        'ffn_conv_b': nrm(ks[18], (L, 2 * D_FF), 0.02),
        'ffn_w_down': nrm(ks[19], (L, D_FF, D_MODEL), D_FF ** -0.5),
        'final_g': 1.0 + nrm(ks[20], (D_MODEL,), 0.02),
    }


def reference(x, c, ln1_g, ln2_g, w_ada, b_ada, w_in, cmp_pos_k, cmp_w1_k, cmp_w2_k,
              cmp_pos_v, cmp_w1_v, cmp_w2_v, sb_out_g, nsa_out_g, w_out,
              ffn_w_in, ffn_conv_w, ffn_conv_b, ffn_w_down, final_g):
    for l in range(DEPTH):
        mod = jax.nn.silu(c) @ w_ada[l] + b_ada[l]
        sh1, sc1, g1, sh2, sc2, g2 = [m[:, None, :] for m in jnp.split(mod, 6, axis=-1)]
        h = rms_norm(x, ln1_g[l]) * (1.0 + sc1) + sh1
        a = token_mixer(h, w_in[l], cmp_pos_k[l], cmp_w1_k[l], cmp_w2_k[l],
                        cmp_pos_v[l], cmp_w1_v[l], cmp_w2_v[l], sb_out_g[l], nsa_out_g[l], w_out[l])
        x = x + (g1 * a).astype(x.dtype)
        h = rms_norm(x, ln2_g[l]) * (1.0 + sc2) + sh2
        f = channel_mixer(h, ffn_w_in[l], ffn_conv_w[l], ffn_conv_b[l], ffn_w_down[l])
        x = x + (g2 * f).astype(x.dtype)
    return rms_norm(x, final_g)
```

```python
import functools
import math

import jax
import jax.numpy as jnp
from jax import lax
from jax.experimental import pallas as pl
from jax.experimental.pallas import tpu as pltpu

D_MODEL = 1024
HEAD_DIM = 64
SB_HEADS = 8
NSA_HEADS = 8
NSA_KV_HEADS = 2
NSA_GROUP = NSA_HEADS // NSA_KV_HEADS
SB_WIDTH = SB_HEADS * HEAD_DIM
NSA_WIDTH = NSA_HEADS * HEAD_DIM
KV_WIDTH = NSA_KV_HEADS * HEAD_DIM
N_BRANCH = 3
CMP_LEN = 32
CMP_STRIDE = 16
CMP_HIDDEN = 256
SEL_LEN = 64
SEL_TOPK = 16
N_LOCAL = 2
WINDOW = 512
ROPE_THETA = 10000.0
D_FF = 2816
CONV_W = 3
EPS = 1e-6
NEG = -1e30
FORCED_SCORE = 1e6

F32 = jnp.float32
MXU_DTYPE = jnp.bfloat16
LANES = 128
SUBLANES = 8
VMEM_LIMIT = 56 * 1024 * 1024
QK_SCALE = HEAD_DIM ** -0.5
EXP_ZERO_BELOW = -104.0

_NT = (((1,), (1,)), ((), ()))


def _cparams(sem):
    return pltpu.CompilerParams(dimension_semantics=sem, vmem_limit_bytes=VMEM_LIMIT)


def _dot(a, b):
    return jnp.dot(a, b, preferred_element_type=F32)


def _dot_nt(a, b):
    return lax.dot_general(a, b, _NT, preferred_element_type=F32)


def _split_hi_lo(a):
    hi = a.astype(MXU_DTYPE)
    lo = (a - hi.astype(F32)).astype(MXU_DTYPE)
    return hi, lo


def _iota(shape, dim):
    return lax.broadcasted_iota(jnp.int32, shape, dim)


def _mod_kernel(c_ref, w_ref, b_ref, o_ref):
    c = c_ref[...]
    a = c * jax.nn.sigmoid(c)
    o_ref[...] = jnp.dot(a, w_ref[...], preferred_element_type=F32,
                         precision=lax.Precision.HIGHEST) + b_ref[...]


def _modulation(c, w_ada, b_ada):
    L, D, D6 = w_ada.shape
    B = c.shape[0]
    n = D6 // D
    return pl.pallas_call(
        _mod_kernel,
        out_shape=jax.ShapeDtypeStruct((L, B, D6), F32),
        grid=(L, n),
        in_specs=[pl.BlockSpec((B, D), lambda l, j: (0, 0)),
                  pl.BlockSpec((None, D, D), lambda l, j: (l, 0, j)),
                  pl.BlockSpec((None, 1, D), lambda l, j: (l, 0, j))],
        out_specs=pl.BlockSpec((None, B, D), lambda l, j: (l, 0, j)),
        compiler_params=_cparams(("parallel", "parallel")),
        name="adaln_mod",
    )(c, w_ada, b_ada.reshape(L, 1, D6))


def _rope_table_kernel(inv_ref, cos_ref, sin_ref, *, rows, stride, offset):
    i = pl.program_id(0)
    r = _iota((rows, LANES), 0) + i * rows
    pos = (r * stride + offset).astype(F32)
    ang = pos * inv_ref[...]
    lane = _iota((rows, LANES), 1)
    first = (lane % HEAD_DIM) < (HEAD_DIM // 2)
    s = jnp.sin(ang)
    cos_ref[...] = jnp.cos(ang)
    sin_ref[...] = jnp.where(first, -s, s)


def _rope_tables(inv_lanes, n_rows, stride, offset):
    rows = min(n_rows, 512)
    kern = functools.partial(_rope_table_kernel, rows=rows, stride=stride, offset=offset)
    return pl.pallas_call(
        kern,
        out_shape=(jax.ShapeDtypeStruct((n_rows, LANES), F32),) * 2,
        grid=(n_rows // rows,),
        in_specs=[pl.BlockSpec((1, LANES), lambda i: (0, 0))],
        out_specs=(pl.BlockSpec((rows, LANES), lambda i: (i, 0)),) * 2,
        compiler_params=_cparams(("parallel",)),
        name="rope_tables",
    )(inv_lanes)


def _apply_rope(v, cos, sin_signed):
    lane = _iota(v.shape, 1)
    first = (lane % HEAD_DIM) < (HEAD_DIM // 2)
    half = HEAD_DIM // 2
    partner = jnp.where(first, pltpu.roll(v, LANES - half, 1), pltpu.roll(v, half, 1))
    return v * cos + partner * sin_signed


_PROJ_COLS = 3 * SB_WIDTH + NSA_WIDTH + 2 * N_BRANCH * KV_WIDTH + LANES


def _proj_kernel(x_ref, mod_ref, g_ref, w_ref, cos_ref, sin_ref,
                 sbq_ref, sbk_ref, sbv_ref, nq_ref, kc_ref, vc_ref,
                 ks_ref, vs_ref, kw_ref, vw_ref, gate_ref):
    x = x_ref[...]
    shift = mod_ref[0:1, :]
    scale = mod_ref[1:2, :]
    ms = jnp.mean(x * x, axis=-1, keepdims=True)
    h = (x * lax.rsqrt(ms + EPS)) * g_ref[...]
    h = h * (1.0 + scale) + shift
    hb = h.astype(MXU_DTYPE)
    cos = cos_ref[...]
    sin = sin_ref[...]

    def mm(lo, width):
        return _dot(hb, w_ref[:, lo:lo + width])

    o = 0
    sbq_ref[...] = (mm(o, SB_WIDTH) * QK_SCALE).astype(sbq_ref.dtype); o += SB_WIDTH
    sbk_ref[...] = mm(o, SB_WIDTH).astype(sbk_ref.dtype); o += SB_WIDTH
    sbv_ref[...] = mm(o, SB_WIDTH).astype(sbv_ref.dtype); o += SB_WIDTH
    nq = mm(o, NSA_WIDTH); o += NSA_WIDTH
    for p in range(NSA_WIDTH // LANES):
        blk = _apply_rope(nq[:, p * LANES:(p + 1) * LANES], cos, sin) * QK_SCALE
        nq_ref[:, p * LANES:(p + 1) * LANES] = blk.astype(nq_ref.dtype)
    rest = mm(o, 2 * N_BRANCH * KV_WIDTH + LANES)
    kc_ref[...] = rest[:, 0 * LANES:1 * LANES].astype(kc_ref.dtype)
    vc_ref[...] = rest[:, 1 * LANES:2 * LANES].astype(vc_ref.dtype)
    ks_ref[...] = _apply_rope(rest[:, 2 * LANES:3 * LANES], cos, sin).astype(ks_ref.dtype)
    vs_ref[...] = rest[:, 3 * LANES:4 * LANES].astype(vs_ref.dtype)
    kw_ref[...] = _apply_rope(rest[:, 4 * LANES:5 * LANES], cos, sin).astype(kw_ref.dtype)
    vw_ref[...] = rest[:, 5 * LANES:6 * LANES].astype(vw_ref.dtype)
    gate_ref[...] = jax.nn.sigmoid(rest[:, 6 * LANES:7 * LANES])


def _projection(x, mod_l, ln_g, w_cat, cos_t, sin_t, tm):
    B, T, D = x.shape
    row = lambda w: pl.BlockSpec((None, tm, w), lambda b, t: (b, t, 0))
    shp = lambda w, dt: jax.ShapeDtypeStruct((B, T, w), dt)
    md = MXU_DTYPE
    return pl.pallas_call(
        _proj_kernel,
        out_shape=(shp(SB_WIDTH, md), shp(SB_WIDTH, md), shp(SB_WIDTH, md), shp(NSA_WIDTH, md),
                   shp(LANES, md), shp(LANES, md), shp(LANES, md), shp(LANES, md),
                   shp(LANES, md), shp(LANES, md), shp(LANES, F32)),
        grid=(B, T // tm),
        in_specs=[row(D),
                  pl.BlockSpec((None, 6, D), lambda b, t: (b, 0, 0)),
                  pl.BlockSpec((1, D), lambda b, t: (0, 0)),
                  pl.BlockSpec((D, _PROJ_COLS), lambda b, t: (0, 0)),
                  pl.BlockSpec((tm, LANES), lambda b, t: (t, 0)),
                  pl.BlockSpec((tm, LANES), lambda b, t: (t, 0))],
        out_specs=(row(SB_WIDTH), row(SB_WIDTH), row(SB_WIDTH), row(NSA_WIDTH),
                   row(LANES), row(LANES), row(LANES), row(LANES), row(LANES), row(LANES),
                   row(LANES)),
        compiler_params=_cparams(("parallel", "parallel")),
        name="norm_proj",
    )(x, mod_l, ln_g, w_cat, cos_t, sin_t)


def _sb_kernel(q_ref, k_ref, v_ref, uo_ref, o_ref, *, tq):
    i = pl.program_id(2)
    q = q_ref[...]
    lane = _iota((tq, LANES), 1)
    rowi = _iota((tq, tq), 0)
    coli = _iota((tq, tq), 1)
    causal = coli < rowi
    uo = uo_ref[...]

    def tile(qh, kb, carry, acc, diag):
        k0 = pl.multiple_of(kb * tq, tq)
        kt = k_ref[pl.ds(k0, tq), :]
        vt = v_ref[pl.ds(k0, tq), :]
        z = _dot_nt(qh, kt)
        lsn = -(jnp.maximum(z, 0.0) + jnp.log(1.0 + jnp.exp(-jnp.abs(z))))
        if diag:
            lsn = jnp.where(causal, lsn, 0.0)
        hi, lo = _split_hi_lo(lsn)
        cs = _dot(hi, uo) + _dot(lo, uo)
        tail = cs[:, :tq] + carry
        w = jnp.exp(z + lsn + tail)
        if diag:
            w = jnp.where(causal, w, 0.0)
        acc = acc + _dot(w.astype(MXU_DTYPE), vt)
        carry = carry + cs[:, tq:]
        return carry, acc

    outs = []
    for hh in range(2):
        in_head = (lane // HEAD_DIM) == hh
        qh = jnp.where(in_head, q, jnp.zeros_like(q))
        carry0 = jnp.zeros((tq, tq), F32)
        acc0 = jnp.zeros((tq, LANES), F32)
        carry1, acc1 = tile(qh, i, carry0, acc0, True)

        def live_flag(carry):
            return (jnp.max(carry) > EXP_ZERO_BELOW).astype(jnp.int32)

        def cond(st):
            kb, live, _, _ = st
            return jnp.logical_and(kb >= 0, live > 0)

        def body(st):
            kb, _, carry, acc = st
            carry, acc = tile(qh, kb, carry, acc, False)
            return kb - 1, live_flag(carry), carry, acc

        _, _, _, acc = lax.while_loop(cond, body, (i - 1, live_flag(carry1), carry1, acc1))
        outs.append(acc)
    o_ref[...] = jnp.where(lane < HEAD_DIM, outs[0], outs[1])


def _sb_attention(sbq, sbk, sbv, tq):
    B, T, W = sbq.shape
    r = jnp.arange(tq)
    u = (r[:, None] > r[None, :]).astype(MXU_DTYPE)
    uo = jnp.concatenate([u, jnp.ones((tq, tq), MXU_DTYPE)], axis=1)
    kern = functools.partial(_sb_kernel, tq=tq)
    return pl.pallas_call(
        kern,
        out_shape=jax.ShapeDtypeStruct((B, T, W), F32),
        grid=(B, W // LANES, T // tq),
        in_specs=[pl.BlockSpec((None, tq, LANES), lambda b, p, i: (b, i, p)),
                  pl.BlockSpec((None, T, LANES), lambda b, p, i: (b, 0, p)),
                  pl.BlockSpec((None, T, LANES), lambda b, p, i: (b, 0, p)),
                  pl.BlockSpec((tq, 2 * tq), lambda b, p, i: (0, 0))],
        out_specs=pl.BlockSpec((None, tq, LANES), lambda b, p, i: (b, i, p)),
        compiler_params=_cparams(("parallel", "parallel", "parallel")),
        name="sb_attention",
    )(sbq, sbk, sbv, uo)


def _gelu_tanh(x):
    c = math.sqrt(2.0 / math.pi)
    return 0.5 * x * (1.0 + jnp.tanh(c * (x + 0.044715 * (x * x * x))))


def _compress_kernel(ck_ref, cv_ref, w1k_ref, w2k_ref, pk_ref, w1v_ref, w2v_ref, pv_ref,
                     cos_ref, sin_ref, ko_ref, vo_ref, *, nch):
    half = CMP_STRIDE * HEAD_DIM
    rowi = _iota((nch, LANES), 0)

    def one(c_ref, w1_ref, w2_ref, p_ref, rope):
        w1 = w1_ref[...]
        bias = _dot(p_ref[...], w1)[0:1, :]
        acc = jnp.zeros((nch, LANES), F32)
        for g in range(NSA_KV_HEADS):
            c = c_ref[g]
            a = _dot(c, w1[:half, :])
            b = _dot(c, w1[half:, :])
            hid = a + pltpu.roll(b, nch - 1, 0) + bias
            acc = acc + _dot(_gelu_tanh(hid).astype(MXU_DTYPE), w2_ref[g])
        if rope:
            acc = _apply_rope(acc, cos_ref[...], sin_ref[...])
        return jnp.where(rowi < nch - 1, acc, 0.0)

    ko_ref[...] = one(ck_ref, w1k_ref, w2k_ref, pk_ref, True).astype(ko_ref.dtype)
    vo_ref[...] = one(cv_ref, w1v_ref, w2v_ref, pv_ref, False).astype(vo_ref.dtype)


def _chunk_flatten(a, nch):
    B = a.shape[0]
    a = a.reshape(B, nch, CMP_STRIDE, NSA_KV_HEADS, HEAD_DIM)
    return a.transpose(0, 3, 1, 2, 4).reshape(B, NSA_KV_HEADS, nch, CMP_STRIDE * HEAD_DIM)


def _pad_w2(w2):
    z = jnp.zeros_like(w2)
    return jnp.stack([jnp.concatenate([w2, z], 1), jnp.concatenate([z, w2], 1)]).astype(MXU_DTYPE)


def _compress(kc, vc, w1k, w2k, pk, w1v, w2v, pv, cos_c, sin_c):
    B, T, _ = kc.shape
    nch = T // CMP_STRIDE
    flat = CMP_STRIDE * HEAD_DIM
    ck = _chunk_flatten(kc, nch)
    cv = _chunk_flatten(vc, nch)
    posf = lambda p: jnp.broadcast_to(p.reshape(1, CMP_LEN * HEAD_DIM), (SUBLANES, CMP_LEN * HEAD_DIM)).astype(MXU_DTYPE)
    full = lambda *s: pl.BlockSpec(s, lambda b: (0,) * len(s))
    kern = functools.partial(_compress_kernel, nch=nch)
    return pl.pallas_call(
        kern,
        out_shape=(jax.ShapeDtypeStruct((B, nch, LANES), MXU_DTYPE),) * 2,
        grid=(B,),
        in_specs=[pl.BlockSpec((None, NSA_KV_HEADS, nch, flat), lambda b: (b, 0, 0, 0)),
                  pl.BlockSpec((None, NSA_KV_HEADS, nch, flat), lambda b: (b, 0, 0, 0)),
                  full(2 * flat, CMP_HIDDEN), full(2, CMP_HIDDEN, LANES), full(SUBLANES, 2 * flat),
                  full(2 * flat, CMP_HIDDEN), full(2, CMP_HIDDEN, LANES), full(SUBLANES, 2 * flat),
                  full(nch, LANES), full(nch, LANES)],
        out_specs=(pl.BlockSpec((None, nch, LANES), lambda b: (b, 0, 0)),) * 2,
        compiler_params=_cparams(("parallel",)),
        name="nsa_compress",
    )(ck, cv, w1k.astype(MXU_DTYPE), _pad_w2(w2k), posf(pk),
      w1v.astype(MXU_DTYPE), _pad_w2(w2v), posf(pv), cos_c, sin_c)


def _softmax_step(state, s, bias, valid, vt):
    m, l, acc = state
    sm = s + bias
    m_new = jnp.maximum(m, jnp.max(sm, axis=1, keepdims=True))
    alpha = jnp.exp(m - m_new)
    p = jnp.exp(sm - m_new) * valid
    l = alpha * l + jnp.sum(p, axis=1, keepdims=True)
    acc = alpha * acc + _dot(p.astype(MXU_DTYPE), vt)
    return m_new, l, acc


def _softmax_init(rows):
    return (jnp.full((rows, 1), NEG, F32), jnp.zeros((rows, 1), F32), jnp.zeros((rows, LANES), F32))


def _softmax_done(state):
    _, l, acc = state
    return acc / jnp.maximum(l, 1e-6)


def _stack4(a):
    return jnp.concatenate([a] * NSA_GROUP, axis=0)


def _mask_terms(valid):
    vf = valid.astype(F32)
    return _stack4(jnp.where(valid, 0.0, NEG)), _stack4(vf)


def _nsa_kernel(q_ref, gate_ref, kcmp_ref, vcmp_ref, ks_ref, vs_ref, kw_ref, vw_ref,
                ovt_ref, esel_ref, o_ref, *, tq, tk, n_sel, nsp):
    i = pl.program_id(1)
    q0 = i * tq
    ncmp = kcmp_ref.shape[0]
    rows = NSA_GROUP * tq
    top = min(SEL_TOPK, n_sel)
    q = q_ref[...]
    gates = gate_ref[...]
    lane = _iota((tq, LANES), 1)
    kcmp = kcmp_ref[...]
    vcmp = vcmp_ref[...]
    outs = [None] * NSA_GROUP

    for g in range(NSA_KV_HEADS):
        in_group = (lane // HEAD_DIM) == g
        qg = jnp.concatenate(
            [jnp.where(in_group, q[:, p * LANES:(p + 1) * LANES], jnp.zeros((tq, LANES), q.dtype))
             for p in range(NSA_GROUP)], axis=0)

        qpos_c = q0 + _iota((tq, ncmp), 0)
        cend_c = _iota((tq, ncmp), 1) * CMP_STRIDE + (CMP_LEN - 1)
        bias_c, valid_c = _mask_terms(cend_c <= qpos_c)
        st = _softmax_step(_softmax_init(rows), _dot_nt(qg, kcmp), bias_c, valid_c, vcmp)
        o_c = _softmax_done(st)

        s_t = _dot_nt(kcmp, qg)
        cend_t = _iota((ncmp, rows), 0) * CMP_STRIDE + (CMP_LEN - 1)
        qpos_t = q0 + _iota((ncmp, rows), 1) % tq
        valid_t = cend_t <= qpos_t
        s_t = jnp.where(valid_t, s_t, NEG)
        m_t = jnp.max(s_t, axis=0, keepdims=True)
        p_t = jnp.where(valid_t, jnp.exp(s_t - m_t), 0.0)
        p_t = p_t / jnp.maximum(jnp.sum(p_t, axis=0, keepdims=True), 1e-6)
        psum = p_t[:, 0:tq]
        for hl in range(1, NSA_GROUP):
            psum = psum + p_t[:, hl * tq:(hl + 1) * tq]
        ph, plo = _split_hi_lo(psum)
        ovt = ovt_ref[...]
        imp_t = _dot(ovt, ph) + _dot(ovt, plo)

        jidx = _iota((nsp, tq), 0)
        qblk = (q0 + _iota((nsp, tq), 1)) // SEL_LEN
        dist = qblk - jidx
        forced = (jidx == 0) | ((dist >= 0) & (dist < N_LOCAL))
        score = jnp.where(dist < 0, -1.0, jnp.where(forced, FORCED_SCORE, imp_t))
        rank = jnp.zeros((nsp, tq), F32)
        for i2 in range(n_sel):
            r = score[i2:i2 + 1, :]
            beats = jnp.where(r > score, 1.0, jnp.where((r == score) & (jidx > i2), 1.0, 0.0))
            rank = rank + beats
        sel_t = jnp.where((rank < top) & (score >= 0.0), 1.0, 0.0)
        if nsp < LANES:
            sel_t = jnp.concatenate([sel_t, jnp.zeros((LANES - nsp, tq), F32)], axis=0)
        selq = sel_t.T.astype(MXU_DTYPE)

        def sel_body(kb, state):
            k0 = pl.multiple_of(kb * tk, tk)
            kt = ks_ref[pl.ds(k0, tk), :]
            vt = vs_ref[pl.ds(k0, tk), :]
            member = _dot(selq, esel_ref[:, pl.ds(k0, tk)])
            kpos = k0 + _iota((tq, tk), 1)
            qpos = q0 + _iota((tq, tk), 0)
            bias, valid = _mask_terms((member > 0.5) & (kpos <= qpos))
            return _softmax_step(state, _dot_nt(qg, kt), bias, valid, vt)

        n_tiles = (q0 + tq + tk - 1) // tk
        o_s = _softmax_done(lax.fori_loop(0, n_tiles, sel_body, _softmax_init(rows)))

        st = _softmax_init(rows)
        for j in range(WINDOW // tq + 1):
            k0 = q0 - WINDOW + j * tq
            k0c = pl.multiple_of(jnp.maximum(k0, 0), tq)
            kt = kw_ref[pl.ds(k0c, tq), :]
            vt = vw_ref[pl.ds(k0c, tq), :]
            kpos = k0 + _iota((tq, tq), 1)
            rel = (q0 + _iota((tq, tq), 0)) - kpos
            bias, valid = _mask_terms((rel >= 0) & (rel < WINDOW) & (kpos >= 0))
            st = _softmax_step(st, _dot_nt(qg, kt), bias, valid, vt)
        o_w = _softmax_done(st)

        for hl in range(NSA_GROUP):
            h = g * NSA_GROUP + hl
            sl = slice(hl * tq, (hl + 1) * tq)
            gc = gates[:, N_BRANCH * h + 0:N_BRANCH * h + 1]
            gs = gates[:, N_BRANCH * h + 1:N_BRANCH * h + 2]
            gw = gates[:, N_BRANCH * h + 2:N_BRANCH * h + 3]
            res = gc * o_c[sl] + gs * o_s[sl] + gw * o_w[sl]
            outs[hl] = res if g == 0 else jnp.where(lane < HEAD_DIM, outs[hl], res)

    for hl in range(NSA_GROUP):
        o_ref[:, hl * LANES:(hl + 1) * LANES] = outs[hl]


def _nsa_attention(nq, gates, kcmp, vcmp, ks, vs, kw, vw, tq, tk):
    B, T, _ = nq.shape
    ncmp = kcmp.shape[1]
    n_sel = T // SEL_LEN
    nsp = -(-n_sel // SUBLANES) * SUBLANES
    cs = jnp.arange(ncmp) * CMP_STRIDE
    ss = jnp.arange(nsp) * SEL_LEN
    ov = jnp.clip(jnp.minimum(cs[None, :] + CMP_LEN, ss[:, None] + SEL_LEN)
                  - jnp.maximum(cs[None, :], ss[:, None]), 0, None).astype(F32) / CMP_LEN
    ov = jnp.where((jnp.arange(nsp) < n_sel)[:, None] & (jnp.arange(ncmp) < ncmp - 1)[None, :], ov, 0.0)
    ovt = ov.astype(MXU_DTYPE)
    esel = (jnp.arange(LANES)[:, None] == (jnp.arange(T) // SEL_LEN)[None, :]).astype(MXU_DTYPE)
    kern = functools.partial(_nsa_kernel, tq=tq, tk=tk, n_sel=n_sel, nsp=nsp)
    per_b = lambda r, w: pl.BlockSpec((None, r, w), lambda b, i: (b, 0, 0))
    return pl.pallas_call(
        kern,
        out_shape=jax.ShapeDtypeStruct((B, T, NSA_WIDTH), F32),
        grid=(B, T // tq),
        in_specs=[pl.BlockSpec((None, tq, NSA_WIDTH), lambda b, i: (b, i, 0)),
                  pl.BlockSpec((None, tq, LANES), lambda b, i: (b, i, 0)),
                  per_b(ncmp, LANES), per_b(ncmp, LANES),
                  per_b(T, LANES), per_b(T, LANES), per_b(T, LANES), per_b(T, LANES),
                  pl.BlockSpec((nsp, ncmp), lambda b, i: (0, 0)),
                  pl.BlockSpec((LANES, T), lambda b, i: (0, 0))],
        out_specs=pl.BlockSpec((None, tq, NSA_WIDTH), lambda b, i: (b, i, 0)),
        compiler_params=_cparams(("parallel", "parallel")),
        name="nsa_attention",
    )(nq, gates, kcmp, vcmp, ks, vs, kw, vw, ovt, esel)


def _rms(v, g):
    return (v * lax.rsqrt(jnp.mean(v * v, axis=-1, keepdims=True) + EPS)) * g


def _outproj_kernel(x_ref, osb_ref, onsa_ref, mod_ref, gsb_ref, gnsa_ref, w_ref, o_ref):
    a = _rms(osb_ref[...], gsb_ref[...]).astype(MXU_DTYPE)
    b = _rms(onsa_ref[...], gnsa_ref[...]).astype(MXU_DTYPE)
    y = _dot(a, w_ref[:SB_WIDTH, :]) + _dot(b, w_ref[SB_WIDTH:, :])
    o_ref[...] = x_ref[...] + mod_ref[2:3, :] * y


def _out_projection(x, o_sb, o_nsa, mod_l, g_sb, g_nsa, w_out, tm):
    B, T, D = x.shape
    row = lambda w: pl.BlockSpec((None, tm, w), lambda b, t: (b, t, 0))
    return pl.pallas_call(
        _outproj_kernel,
        out_shape=jax.ShapeDtypeStruct((B, T, D), F32),
        grid=(B, T // tm),
        in_specs=[row(D), row(SB_WIDTH), row(NSA_WIDTH),
                  pl.BlockSpec((None, 6, D), lambda b, t: (b, 0, 0)),
                  pl.BlockSpec((1, SB_WIDTH), lambda b, t: (0, 0)),
                  pl.BlockSpec((1, NSA_WIDTH), lambda b, t: (0, 0)),
                  pl.BlockSpec((D, D), lambda b, t: (0, 0))],
        out_specs=row(D),
        compiler_params=_cparams(("parallel", "parallel")),
        name="out_proj",
    )(x, o_sb, o_nsa, mod_l, g_sb, g_nsa, w_out)


def _ffn_kernel(x_ref, mod_ref, g_ref, win_ref, cw_ref, wdn_ref, fg_ref, o_ref, prev_ref,
                *, tm, fc, final_norm):
    t = pl.program_id(1)

    @pl.when(t == 0)
    def _():
        prev_ref[...] = jnp.zeros_like(prev_ref)

    x = x_ref[...]
    h = _rms(x, g_ref[...]) * (1.0 + mod_ref[4:5, :]) + mod_ref[3:4, :]
    hb = h.astype(MXU_DTYPE)
    rowi = _iota((tm, fc), 0)

    def conv(u, c0):
        pm1 = prev_ref[SUBLANES - 1:SUBLANES, c0:c0 + fc]
        pm2 = prev_ref[SUBLANES - 2:SUBLANES - 1, c0:c0 + fc]
        u1 = jnp.where(rowi == 0, pm1, pltpu.roll(u, 1, 0))
        u2 = jnp.where(rowi == 0, pm2, jnp.where(rowi == 1, pm1, pltpu.roll(u, 2, 0)))
        prev_ref[:, c0:c0 + fc] = u[tm - SUBLANES:, :]
        cw = cw_ref[:, c0:c0 + fc]
        return cw[2:3, :] * u + cw[1:2, :] * u1 + cw[0:1, :] * u2 + cw[3:4, :]

    acc = jnp.zeros((tm, D_MODEL), F32)
    for c in range(D_FF // fc):
        ca = c * fc
        cb = D_FF + c * fc
        ya = conv(_dot(hb, win_ref[:, ca:ca + fc]), ca)
        yb = conv(_dot(hb, win_ref[:, cb:cb + fc]), cb)
        z = (ya * jax.nn.sigmoid(ya)) * yb
        acc = acc + _dot(z.astype(MXU_DTYPE), wdn_ref[ca:ca + fc, :])
    y = x + mod_ref[5:6, :] * acc
    if final_norm:
        y = _rms(y, fg_ref[...])
    o_ref[...] = y


def _ffn(x, mod_l, ln_g, w_in, conv_wb, w_down, final_g, tm, fc, final_norm):
    B, T, D = x.shape
    row = pl.BlockSpec((None, tm, D), lambda b, t: (b, t, 0))
    const = lambda r, c: pl.BlockSpec((r, c), lambda b, t: (0, 0), pipeline_mode=pl.Buffered(1))
    kern = functools.partial(_ffn_kernel, tm=tm, fc=fc, final_norm=final_norm)
    return pl.pallas_call(
        kern,
        out_shape=jax.ShapeDtypeStruct((B, T, D), F32),
        grid=(B, T // tm),
        in_specs=[row,
                  pl.BlockSpec((None, 6, D), lambda b, t: (b, 0, 0)),
                  pl.BlockSpec((1, D), lambda b, t: (0, 0)),
                  const(D, 2 * D_FF), const(SUBLANES, 2 * D_FF), const(D_FF, D),
                  pl.BlockSpec((1, D), lambda b, t: (0, 0))],
        out_specs=row,
        scratch_shapes=[pltpu.VMEM((SUBLANES, 2 * D_FF), F32)],
        compiler_params=_cparams(("parallel", "arbitrary")),
        name="conv_ffn",
    )(x, mod_l, ln_g, w_in, conv_wb, w_down, final_g)


_NSA_HEAD_ORDER = [h for p in range(NSA_GROUP) for h in (p, NSA_GROUP + p)]


def _head_cols(order):
    return jnp.asarray([h * HEAD_DIM + d for h in order for d in range(HEAD_DIM)], jnp.int32)


def _prep_w_in(w):
    nq0 = 3 * SB_WIDTH
    kv0 = nq0 + NSA_WIDTH
    gl0 = kv0 + 2 * N_BRANCH * KV_WIDTH
    nq = w[:, nq0:kv0][:, _head_cols(_NSA_HEAD_ORDER)]
    gl = jnp.pad(w[:, gl0:], ((0, 0), (0, LANES - (w.shape[1] - gl0))))
    return jnp.concatenate([w[:, :nq0], nq, w[:, kv0:gl0], gl], axis=1).astype(MXU_DTYPE)


def _tiles(T):
    tm = 256 if T % 256 == 0 else 128
    return tm


def kernel(x, c, ln1_g, ln2_g, w_ada, b_ada, w_in, cmp_pos_k, cmp_w1_k, cmp_w2_k, cmp_pos_v, cmp_w1_v, cmp_w2_v, sb_out_g, nsa_out_g, w_out, ffn_w_in, ffn_conv_w, ffn_conv_b, ffn_w_down, final_g):
    B, T, D = x.shape
    L = w_in.shape[0]
    assert D == D_MODEL and T % 256 == 0 and T // SEL_LEN <= LANES
    tm = 256
    tq = 128
    nch = T // CMP_STRIDE

    half = HEAD_DIM // 2
    inv = ROPE_THETA ** (-jnp.arange(half, dtype=F32) / half)
    inv_lanes = jnp.tile(inv, LANES // half).reshape(1, LANES)
    cos_t, sin_t = _rope_tables(inv_lanes, T, 1, 0)
    cos_c, sin_c = _rope_tables(inv_lanes, nch, CMP_STRIDE, CMP_LEN - 1)

    mod = _modulation(c, w_ada, b_ada).reshape(L, B, 6, D)
    nsa_cols = _head_cols(_NSA_HEAD_ORDER)

    for l in range(L):
        mod_l = mod[l]
        (sbq, sbk, sbv, nq, kc, vc, ks, vs, kw, vw, gates) = _projection(
            x, mod_l, ln1_g[l].reshape(1, D), _prep_w_in(w_in[l]), cos_t, sin_t, tm)
        o_sb = _sb_attention(sbq, sbk, sbv, tq)
        kcmp, vcmp = _compress(kc, vc, cmp_w1_k[l], cmp_w2_k[l], cmp_pos_k[l],
                               cmp_w1_v[l], cmp_w2_v[l], cmp_pos_v[l], cos_c, sin_c)
        o_nsa = _nsa_attention(nq, gates, kcmp, vcmp, ks, vs, kw, vw, tq, 256)
        w_o = jnp.concatenate([w_out[l][:SB_WIDTH], w_out[l][SB_WIDTH:][nsa_cols]], axis=0).astype(MXU_DTYPE)
        x = _out_projection(x, o_sb, o_nsa, mod_l, sb_out_g[l].reshape(1, SB_WIDTH),
                            nsa_out_g[l][nsa_cols].reshape(1, NSA_WIDTH), w_o, tm)
        conv_wb = jnp.concatenate([ffn_conv_w[l], ffn_conv_b[l][None, :],
                                   jnp.zeros((SUBLANES - CONV_W - 1, 2 * D_FF), F32)], axis=0)
        x = _ffn(x, mod_l, ln2_g[l].reshape(1, D), ffn_w_in[l].astype(MXU_DTYPE), conv_wb,
                 ffn_w_down[l].astype(MXU_DTYPE), final_g.reshape(1, D), tm, 256, l == L - 1)
    return x
```

```python
import functools
import math

import jax
import jax.numpy as jnp
from jax import lax
from jax.experimental import pallas as pl
from jax.experimental.pallas import tpu as pltpu

D_MODEL = 1024
HEAD_DIM = 64
SB_HEADS = 8
NSA_HEADS = 8
NSA_KV_HEADS = 2
NSA_GROUP = NSA_HEADS // NSA_KV_HEADS
SB_WIDTH = SB_HEADS * HEAD_DIM
NSA_WIDTH = NSA_HEADS * HEAD_DIM
KV_WIDTH = NSA_KV_HEADS * HEAD_DIM
N_BRANCH = 3
CMP_LEN = 32
CMP_STRIDE = 16
CMP_HIDDEN = 256
SEL_LEN = 64
SEL_TOPK = 16
N_LOCAL = 2
WINDOW = 512
ROPE_THETA = 10000.0
D_FF = 2816
CONV_W = 3
EPS = 1e-6
NEG = -1e30
FORCED_SCORE = 1e6

F32 = jnp.float32
MXU_DTYPE = jnp.bfloat16
LANES = 128
SUBLANES = 8
VMEM_LIMIT = 56 * 1024 * 1024
QK_SCALE = HEAD_DIM ** -0.5
EXP_ZERO_BELOW = -104.0

_NT = (((1,), (1,)), ((), ()))


def _cparams(sem):
    return pltpu.CompilerParams(dimension_semantics=sem, vmem_limit_bytes=VMEM_LIMIT)


def _dot(a, b):
    return jnp.dot(a, b, preferred_element_type=F32)


def _dot_nt(a, b):
    return lax.dot_general(a, b, _NT, preferred_element_type=F32)


def _split_hi_lo(a):
    hi = a.astype(MXU_DTYPE)
    lo = (a - hi.astype(F32)).astype(MXU_DTYPE)
    return hi, lo


def _iota(shape, dim):
    return lax.broadcasted_iota(jnp.int32, shape, dim)


def _mod_kernel(c_ref, w_ref, b_ref, o_ref):
    c = c_ref[...]
    a = c * jax.nn.sigmoid(c)
    o_ref[...] = jnp.dot(a, w_ref[...], preferred_element_type=F32,
                         precision=lax.Precision.HIGHEST) + b_ref[...]


def _modulation(c, w_ada, b_ada):
    L, D, D6 = w_ada.shape
    B = c.shape[0]
    n = D6 // D
    return pl.pallas_call(
        _mod_kernel,
        out_shape=jax.ShapeDtypeStruct((L, B, D6), F32),
        grid=(L, n),
        in_specs=[pl.BlockSpec((B, D), lambda l, j: (0, 0)),
                  pl.BlockSpec((None, D, D), lambda l, j: (l, 0, j)),
                  pl.BlockSpec((None, 1, D), lambda l, j: (l, 0, j))],
        out_specs=pl.BlockSpec((None, B, D), lambda l, j: (l, 0, j)),
        compiler_params=_cparams(("parallel", "parallel")),
        name="adaln_mod",
    )(c, w_ada, b_ada.reshape(L, 1, D6))


def _rope_table_kernel(inv_ref, cos_ref, sin_ref, *, rows, stride, offset):
    i = pl.program_id(0)
    r = _iota((rows, LANES), 0) + i * rows
    pos = (r * stride + offset).astype(F32)
    ang = pos * inv_ref[...]
    lane = _iota((rows, LANES), 1)
    first = (lane % HEAD_DIM) < (HEAD_DIM // 2)
    s = jnp.sin(ang)
    cos_ref[...] = jnp.cos(ang)
    sin_ref[...] = jnp.where(first, -s, s)


def _rope_tables(inv_lanes, n_rows, stride, offset):
    rows = min(n_rows, 512)
    kern = functools.partial(_rope_table_kernel, rows=rows, stride=stride, offset=offset)
    return pl.pallas_call(
        kern,
        out_shape=(jax.ShapeDtypeStruct((n_rows, LANES), F32),) * 2,
        grid=(n_rows // rows,),
        in_specs=[pl.BlockSpec((1, LANES), lambda i: (0, 0))],
        out_specs=(pl.BlockSpec((rows, LANES), lambda i: (i, 0)),) * 2,
        compiler_params=_cparams(("parallel",)),
        name="rope_tables",
    )(inv_lanes)


def _apply_rope(v, cos, sin_signed):
    lane = _iota(v.shape, 1)
    first = (lane % HEAD_DIM) < (HEAD_DIM // 2)
    half = HEAD_DIM // 2
    partner = jnp.where(first, pltpu.roll(v, LANES - half, 1), pltpu.roll(v, half, 1))
    return v * cos + partner * sin_signed


_PROJ_COLS = 3 * SB_WIDTH + NSA_WIDTH + 2 * N_BRANCH * KV_WIDTH + LANES


def _proj_kernel(x_ref, mod_ref, g_ref, w_ref, cos_ref, sin_ref,
                 sbq_ref, sbk_ref, sbv_ref, nq_ref, kc_ref, vc_ref,
                 ks_ref, vs_ref, kw_ref, vw_ref, gate_ref):
    x = x_ref[...]
    shift = mod_ref[0:1, :]
    scale = mod_ref[1:2, :]
    ms = jnp.mean(x * x, axis=-1, keepdims=True)
    h = (x * lax.rsqrt(ms + EPS)) * g_ref[...]
    h = h * (1.0 + scale) + shift
    hb = h.astype(MXU_DTYPE)
    cos = cos_ref[...]
    sin = sin_ref[...]

    def mm(lo, width):
        return _dot(hb, w_ref[:, lo:lo + width])

    o = 0
    sbq_ref[...] = (mm(o, SB_WIDTH) * QK_SCALE).astype(sbq_ref.dtype); o += SB_WIDTH
    sbk_ref[...] = mm(o, SB_WIDTH).astype(sbk_ref.dtype); o += SB_WIDTH
    sbv_ref[...] = mm(o, SB_WIDTH).astype(sbv_ref.dtype); o += SB_WIDTH
    nq = mm(o, NSA_WIDTH); o += NSA_WIDTH
    for p in range(NSA_WIDTH // LANES):
        blk = _apply_rope(nq[:, p * LANES:(p + 1) * LANES], cos, sin) * QK_SCALE
        nq_ref[:, p * LANES:(p + 1) * LANES] = blk.astype(nq_ref.dtype)
    rest = mm(o, 2 * N_BRANCH * KV_WIDTH + LANES)
    kc_ref[...] = rest[:, 0 * LANES:1 * LANES].astype(kc_ref.dtype)
    vc_ref[...] = rest[:, 1 * LANES:2 * LANES].astype(vc_ref.dtype)
    ks_ref[...] = _apply_rope(rest[:, 2 * LANES:3 * LANES], cos, sin).astype(ks_ref.dtype)
    kw_ref[...] = _apply_rope(rest[:, 4 * LANES:5 * LANES], cos, sin).astype(kw_ref.dtype)
    vs_ref[...] = rest[:, 3 * LANES:4 * LANES].T.astype(vs_ref.dtype)
    vw_ref[...] = rest[:, 5 * LANES:6 * LANES].T.astype(vw_ref.dtype)
    gate_ref[...] = jax.nn.sigmoid(rest[:, 6 * LANES:7 * LANES]).T


def _projection(x, mod_l, ln_g, w_cat, cos_t, sin_t, tm):
    B, T, D = x.shape
    row = lambda w: pl.BlockSpec((None, tm, w), lambda b, t: (b, t, 0))
    col = pl.BlockSpec((None, LANES, tm), lambda b, t: (b, 0, t))
    shp = lambda w, dt: jax.ShapeDtypeStruct((B, T, w), dt)
    shp_t = lambda dt: jax.ShapeDtypeStruct((B, LANES, T), dt)
    md = MXU_DTYPE
    return pl.pallas_call(
        _proj_kernel,
        out_shape=(shp(SB_WIDTH, md), shp(SB_WIDTH, md), shp(SB_WIDTH, md), shp(NSA_WIDTH, md),
                   shp(LANES, md), shp(LANES, md), shp(LANES, md), shp_t(md),
                   shp(LANES, md), shp_t(md), shp_t(F32)),
        grid=(B, T // tm),
        in_specs=[row(D),
                  pl.BlockSpec((None, 6, D), lambda b, t: (b, 0, 0)),
                  pl.BlockSpec((1, D), lambda b, t: (0, 0)),
                  pl.BlockSpec((D, _PROJ_COLS), lambda b, t: (0, 0)),
                  pl.BlockSpec((tm, LANES), lambda b, t: (t, 0)),
                  pl.BlockSpec((tm, LANES), lambda b, t: (t, 0))],
        out_specs=(row(SB_WIDTH), row(SB_WIDTH), row(SB_WIDTH), row(NSA_WIDTH),
                   row(LANES), row(LANES), row(LANES), col, row(LANES), col, col),
        compiler_params=_cparams(("parallel", "parallel")),
        name="norm_proj",
    )(x, mod_l, ln_g, w_cat, cos_t, sin_t)


def _sb_kernel(q_ref, k_ref, v_ref, uo_ref, o_ref, carry_ref, *, tq):
    i = pl.program_id(1)
    n_pairs = q_ref.shape[1] // LANES
    low = _iota((tq, LANES), 1) < HEAD_DIM
    causal = _iota((tq, tq), 1) < _iota((tq, tq), 0)

    def sweep(kb, diag):
        k0 = pl.multiple_of(kb * tq, tq)
        pairs = [slice(p * LANES, (p + 1) * LANES) for p in range(n_pairs)]
        heads = range(2 * n_pairs)
        old_carry = [None if diag else carry_ref[h] for h in heads]
        old_out = [None if diag else o_ref[:, sl] for sl in pairs]
        uo = uo_ref[...]
        zs = []
        for sl in pairs:
            q = q_ref[:, sl]
            kt = k_ref[pl.ds(k0, tq), sl]
            zq = jnp.zeros_like(q)
            zs.append(_dot_nt(jnp.where(low, q, zq), kt))
            zs.append(_dot_nt(jnp.where(low, zq, q), kt))
        lsns, sums = [], []
        for z in zs:
            lsn = -(jnp.maximum(z, 0.0) + jnp.log(1.0 + jnp.exp(-jnp.abs(z))))
            if diag:
                lsn = jnp.where(causal, lsn, 0.0)
            hi, lo = _split_hi_lo(lsn)
            lsns.append(lsn)
            sums.append(_dot(hi, uo) + _dot(lo, uo))
        ws, new_carry = [], []
        for h in heads:
            tail = sums[h][:, :tq]
            total = sums[h][:, tq:]
            if not diag:
                tail = tail + old_carry[h]
                total = total + old_carry[h]
            w = jnp.exp(zs[h] + lsns[h] + tail)
            if diag:
                w = jnp.where(causal, w, 0.0)
            ws.append(w.astype(MXU_DTYPE))
            new_carry.append(total)
        new_out = []
        for p, sl in enumerate(pairs):
            vt = v_ref[pl.ds(k0, tq), sl]
            zv = jnp.zeros_like(vt)
            v2 = jnp.concatenate([jnp.where(low, vt, zv), jnp.where(low, zv, vt)], axis=0)
            upd = _dot(jnp.concatenate(ws[2 * p:2 * p + 2], axis=1), v2)
            new_out.append(upd if diag else old_out[p] + upd)
        worst = new_carry[0]
        for h in heads:
            carry_ref[h] = new_carry[h]
            worst = jnp.maximum(worst, new_carry[h])
        for p, sl in enumerate(pairs):
            o_ref[:, sl] = new_out[p]
        return (jnp.max(worst) > EXP_ZERO_BELOW).astype(jnp.int32)

    def cond(st):
        kb, live = st
        return jnp.logical_and(kb >= 0, live > 0)

    def body(st):
        kb, _ = st
        return kb - 1, sweep(kb, False)

    lax.while_loop(cond, body, (i - 1, sweep(i, True)))


def _sb_attention(sbq, sbk, sbv, tq):
    B, T, W = sbq.shape
    r = jnp.arange(tq)
    u = (r[:, None] > r[None, :]).astype(MXU_DTYPE)
    uo = jnp.concatenate([u, jnp.ones((tq, tq), MXU_DTYPE)], axis=1)
    kern = functools.partial(_sb_kernel, tq=tq)
    return pl.pallas_call(
        kern,
        out_shape=jax.ShapeDtypeStruct((B, T, W), F32),
        grid=(B, T // tq),
        in_specs=[pl.BlockSpec((None, tq, W), lambda b, i: (b, i, 0)),
                  pl.BlockSpec((None, T, W), lambda b, i: (b, 0, 0)),
                  pl.BlockSpec((None, T, W), lambda b, i: (b, 0, 0)),
                  pl.BlockSpec((tq, 2 * tq), lambda b, i: (0, 0))],
        out_specs=pl.BlockSpec((None, tq, W), lambda b, i: (b, i, 0)),
        scratch_shapes=[pltpu.VMEM((W // HEAD_DIM, tq, tq), F32)],
        compiler_params=_cparams(("parallel", "parallel")),
        name="sb_attention",
    )(sbq, sbk, sbv, uo)


def _gelu_tanh(x):
    c = math.sqrt(2.0 / math.pi)
    return 0.5 * x * (1.0 + jnp.tanh(c * (x + 0.044715 * (x * x * x))))


def _compress_kernel(ck_ref, cv_ref, w1k_ref, w2k_ref, pk_ref, w1v_ref, w2v_ref, pv_ref,
                     cos_ref, sin_ref, ko_ref, vo_ref, *, nch):
    half = CMP_STRIDE * HEAD_DIM
    rowi = _iota((nch, LANES), 0)

    def one(c_ref, w1_ref, w2_ref, p_ref, rope):
        w1 = w1_ref[...]
        bias = _dot(p_ref[...], w1)[0:1, :]
        acc = jnp.zeros((nch, LANES), F32)
        for g in range(NSA_KV_HEADS):
            c = c_ref[g]
            a = _dot(c, w1[:half, :])
            b = _dot(c, w1[half:, :])
            hid = a + pltpu.roll(b, nch - 1, 0) + bias
            acc = acc + _dot(_gelu_tanh(hid).astype(MXU_DTYPE), w2_ref[g])
        if rope:
            acc = _apply_rope(acc, cos_ref[...], sin_ref[...])
        return jnp.where(rowi < nch - 1, acc, 0.0)

    ko_ref[...] = one(ck_ref, w1k_ref, w2k_ref, pk_ref, True).astype(ko_ref.dtype)
    vo_ref[...] = one(cv_ref, w1v_ref, w2v_ref, pv_ref, False).T.astype(vo_ref.dtype)


def _chunk_flatten(a, nch):
    B = a.shape[0]
    a = a.reshape(B, nch, CMP_STRIDE, NSA_KV_HEADS, HEAD_DIM)
    return a.transpose(0, 3, 1, 2, 4).reshape(B, NSA_KV_HEADS, nch, CMP_STRIDE * HEAD_DIM)


def _pad_w2(w2):
    z = jnp.zeros_like(w2)
    return jnp.stack([jnp.concatenate([w2, z], 1), jnp.concatenate([z, w2], 1)]).astype(MXU_DTYPE)


def _compress(kc, vc, w1k, w2k, pk, w1v, w2v, pv, cos_c, sin_c):
    B, T, _ = kc.shape
    nch = T // CMP_STRIDE
    flat = CMP_STRIDE * HEAD_DIM
    ck = _chunk_flatten(kc, nch)
    cv = _chunk_flatten(vc, nch)
    posf = lambda p: jnp.broadcast_to(p.reshape(1, CMP_LEN * HEAD_DIM), (SUBLANES, CMP_LEN * HEAD_DIM)).astype(MXU_DTYPE)
    full = lambda *s: pl.BlockSpec(s, lambda b: (0,) * len(s))
    kern = functools.partial(_compress_kernel, nch=nch)
    return pl.pallas_call(
        kern,
        out_shape=(jax.ShapeDtypeStruct((B, nch, LANES), MXU_DTYPE),
                   jax.ShapeDtypeStruct((B, LANES, nch), MXU_DTYPE)),
        grid=(B,),
        in_specs=[pl.BlockSpec((None, NSA_KV_HEADS, nch, flat), lambda b: (b, 0, 0, 0)),
                  pl.BlockSpec((None, NSA_KV_HEADS, nch, flat), lambda b: (b, 0, 0, 0)),
                  full(2 * flat, CMP_HIDDEN), full(2, CMP_HIDDEN, LANES), full(SUBLANES, 2 * flat),
                  full(2 * flat, CMP_HIDDEN), full(2, CMP_HIDDEN, LANES), full(SUBLANES, 2 * flat),
                  full(nch, LANES), full(nch, LANES)],
        out_specs=(pl.BlockSpec((None, nch, LANES), lambda b: (b, 0, 0)),
                   pl.BlockSpec((None, LANES, nch), lambda b: (b, 0, 0))),
        compiler_params=_cparams(("parallel",)),
        name="nsa_compress",
    )(ck, cv, w1k.astype(MXU_DTYPE), _pad_w2(w2k), posf(pk),
      w1v.astype(MXU_DTYPE), _pad_w2(w2v), posf(pv), cos_c, sin_c)


def _nsa_kernel(q_ref, gt_ref, kcmp_ref, vcmpt_ref, ks_ref, vst_ref, kw_ref, vwt_ref,
                ovt_ref, eselt_ref, o_ref, qa_ref, *, tq, n_sel, nsp):
    i = pl.program_id(1)
    q0 = i * tq
    ncmp = kcmp_ref.shape[0]
    top = min(SEL_TOPK, n_sel)
    wide = 2 * tq
    heads = range(NSA_HEADS)
    lane = _iota((tq, LANES), 1)

    for g in range(NSA_KV_HEADS):
        in_group = (lane // HEAD_DIM) == g
        for p in range(NSA_GROUP):
            blk = q_ref[:, p * LANES:(p + 1) * LANES]
            qa_ref[g, p * tq:(p + 1) * tq, :LANES] = jnp.where(in_group, blk, jnp.zeros_like(blk))

    def fresh_state():
        return [(jnp.full((1, tq), NEG, F32), jnp.zeros((HEAD_DIM + SUBLANES, tq), F32)) for _ in heads]

    def flash(state, s_ts, bias, vt_ref, k0, w):
        new = []
        for g in range(NSA_KV_HEADS):
            vt = jnp.concatenate([vt_ref[g * HEAD_DIM:(g + 1) * HEAD_DIM, pl.ds(k0, w)],
                                  jnp.ones((SUBLANES, w), MXU_DTYPE)], axis=0)
            for hl in range(NSA_GROUP):
                m_old, acc_old = state[g * NSA_GROUP + hl]
                x = s_ts[g][:, hl * tq:(hl + 1) * tq]
                if bias is not None:
                    x = x + bias
                m_new = jnp.maximum(m_old, jnp.max(x, axis=0, keepdims=True))
                p = jnp.exp(x - m_new).astype(MXU_DTYPE)
                new.append((m_new, jnp.exp(m_old - m_new) * acc_old + _dot(vt, p)))
        return new

    def gated(state, branch):
        res = []
        for h in heads:
            acc = state[h][1]
            r = N_BRANCH * h + branch
            scale = gt_ref[r:r + 1, :] * (1.0 / jnp.maximum(acc[HEAD_DIM:HEAD_DIM + 1, :], 1e-6))
            res.append(acc[:HEAD_DIM, :] * scale)
        return res

    kcmp = kcmp_ref[...]
    valid_c = (_iota((ncmp, tq), 0) * CMP_STRIDE + (CMP_LEN - 1)) <= (q0 + _iota((ncmp, tq), 1))
    bias_c = jnp.where(valid_c, 0.0, NEG)
    keep_c = jnp.where(valid_c, 1.0, 0.0)
    n_chunks = nsp // SUBLANES
    jrow = _iota((SUBLANES, tq), 0)
    qblk = (q0 + _iota((SUBLANES, tq), 1)) // SEL_LEN
    outs = [None] * NSA_HEADS
    for g in range(NSA_KV_HEADS):
        s_t = _dot_nt(kcmp, qa_ref[g, :, :LANES])
        vct = vcmpt_ref[g * HEAD_DIM:(g + 1) * HEAD_DIM, :]
        psum = None
        for hl in range(NSA_GROUP):
            h = g * NSA_GROUP + hl
            x = s_t[:, hl * tq:(hl + 1) * tq] + bias_c
            p = jnp.exp(x - jnp.max(x, axis=0, keepdims=True)) * keep_c
            p = p * (1.0 / jnp.maximum(jnp.sum(p, axis=0, keepdims=True), 1e-6))
            psum = p if psum is None else psum + p
            outs[h] = gt_ref[N_BRANCH * h:N_BRANCH * h + 1, :] * _dot(vct, p.astype(MXU_DTYPE))
        ph, plo = _split_hi_lo(psum)
        ovt = ovt_ref[...]
        imp_t = _dot(ovt, ph) + _dot(ovt, plo)
        scores = []
        for c in range(n_chunks):
            dist = qblk - (jrow + c * SUBLANES)
            forced = (dist >= 0) & (dist < N_LOCAL)
            if c == 0:
                forced = forced | (jrow == 0)
            imp_c = imp_t[c * SUBLANES:(c + 1) * SUBLANES, :]
            scores.append(jnp.where(dist < 0, -1.0, jnp.where(forced, FORCED_SCORE, imp_c)))
        ranks = [jnp.zeros((SUBLANES, tq), F32) for _ in range(n_chunks)]
        for i2 in range(n_sel):
            c2, r2 = divmod(i2, SUBLANES)
            r = jnp.broadcast_to(scores[c2][r2:r2 + 1, :], (SUBLANES, tq))
            for c in range(n_chunks):
                if c < c2:
                    beats = r > scores[c]
                elif c > c2:
                    beats = r >= scores[c]
                else:
                    beats = (r > scores[c]) | ((r == scores[c]) & (jrow > r2))
                ranks[c] = ranks[c] + jnp.where(beats, 1.0, 0.0)
        nsel = [jnp.where((ranks[c] < top) & (scores[c] >= 0.0), 0.0, NEG) for c in range(n_chunks)]
        if nsp < LANES:
            nsel.append(jnp.zeros((LANES - nsp, tq), F32))
        nsel_q = jnp.concatenate(nsel, axis=0).T.astype(MXU_DTYPE)
        for p in range(NSA_GROUP):
            qa_ref[g, p * tq:(p + 1) * tq, LANES:] = nsel_q

    def sel_scores(t):
        k0 = pl.multiple_of(t * wide, wide)
        kaug = jnp.concatenate([ks_ref[pl.ds(k0, wide), :], eselt_ref[pl.ds(k0, wide), :]], axis=1)
        return [_dot_nt(kaug, qa_ref[g]) for g in range(NSA_KV_HEADS)]

    def sel_body(t, carry):
        s_ts, state = carry
        s_next = sel_scores(t + 1)
        state = flash(state, s_ts, None, vst_ref, pl.multiple_of(t * wide, wide), wide)
        return s_next, state

    n_full = i // 2
    s_last, sel_state = lax.fori_loop(0, n_full, sel_body, (sel_scores(0), fresh_state()))

    n_win = WINDOW // tq

    def win_scores(j):
        k0 = pl.multiple_of(jnp.maximum(q0 - (n_win - j) * tq, 0), tq)
        kt = kw_ref[pl.ds(k0, tq), :]
        return [_dot_nt(kt, qa_ref[g, :, :LANES]) for g in range(NSA_KV_HEADS)], k0

    krow = _iota((tq, tq), 0)
    qcol = _iota((tq, tq), 1)
    causal_bias = jnp.where(krow <= qcol, 0.0, NEG)
    oldest_bias = jnp.where(krow > qcol, 0.0, NEG)

    last_bias = jnp.where(_iota((wide, tq), 0) - _iota((wide, tq), 1) <= (i % 2) * tq, 0.0, NEG)
    w_scores, w_k0 = win_scores(0)
    sel_state = flash(sel_state, s_last, last_bias, vst_ref, pl.multiple_of(n_full * wide, wide), wide)
    for h, res in enumerate(gated(sel_state, 1)):
        outs[h] = outs[h] + res

    win_state = fresh_state()
    for j in range(n_win + 1):
        nxt = win_scores(j + 1) if j < n_win else None
        bias = oldest_bias if j == 0 else (causal_bias if j == n_win else None)
        new_state = flash(win_state, w_scores, bias, vwt_ref, w_k0, tq)
        if j < n_win:
            live = i >= n_win - j
            new_state = [(jnp.where(live, mn, mo), jnp.where(live, an, ao))
                         for (mn, an), (mo, ao) in zip(new_state, win_state)]
            w_scores, w_k0 = nxt
        win_state = new_state
    for h, res in enumerate(gated(win_state, 2)):
        outs[h] = outs[h] + res

    for hl in range(NSA_GROUP):
        pair_t = jnp.concatenate([outs[hl], outs[NSA_GROUP + hl]], axis=0)
        o_ref[:, hl * LANES:(hl + 1) * LANES] = pair_t.T


def _nsa_attention(nq, gates_t, kcmp, vcmp_t, ks, vs_t, kw, vw_t, tq):
    B, T, _ = nq.shape
    ncmp = kcmp.shape[1]
    n_sel = T // SEL_LEN
    nsp = -(-n_sel // SUBLANES) * SUBLANES
    cs = jnp.arange(ncmp) * CMP_STRIDE
    ss = jnp.arange(nsp) * SEL_LEN
    ov = jnp.clip(jnp.minimum(cs[None, :] + CMP_LEN, ss[:, None] + SEL_LEN)
                  - jnp.maximum(cs[None, :], ss[:, None]), 0, None).astype(F32) / CMP_LEN
    ov = jnp.where((jnp.arange(nsp) < n_sel)[:, None] & (jnp.arange(ncmp) < ncmp - 1)[None, :], ov, 0.0)
    ovt = ov.astype(MXU_DTYPE)
    esel_t = ((jnp.arange(T) // SEL_LEN)[:, None] == jnp.arange(LANES)[None, :]).astype(MXU_DTYPE)
    kern = functools.partial(_nsa_kernel, tq=tq, n_sel=n_sel, nsp=nsp)
    per_b = lambda r, w: pl.BlockSpec((None, r, w), lambda b, i: (b, 0, 0))
    return pl.pallas_call(
        kern,
        out_shape=jax.ShapeDtypeStruct((B, T, NSA_WIDTH), F32),
        grid=(B, T // tq),
        in_specs=[pl.BlockSpec((None, tq, NSA_WIDTH), lambda b, i: (b, i, 0)),
                  pl.BlockSpec((None, LANES, tq), lambda b, i: (b, 0, i)),
                  per_b(ncmp, LANES), per_b(LANES, ncmp),
                  per_b(T, LANES), per_b(LANES, T), per_b(T, LANES), per_b(LANES, T),
                  pl.BlockSpec((nsp, ncmp), lambda b, i: (0, 0)),
                  pl.BlockSpec((T, LANES), lambda b, i: (0, 0))],
        out_specs=pl.BlockSpec((None, tq, NSA_WIDTH), lambda b, i: (b, i, 0)),
        scratch_shapes=[pltpu.VMEM((NSA_KV_HEADS, NSA_GROUP * tq, 2 * LANES), MXU_DTYPE)],
        compiler_params=_cparams(("parallel", "parallel")),
        name="nsa_attention",
    )(nq, gates_t, kcmp, vcmp_t, ks, vs_t, kw, vw_t, ovt, esel_t)


def _rms(v, g):
    return (v * lax.rsqrt(jnp.mean(v * v, axis=-1, keepdims=True) + EPS)) * g


def _outproj_kernel(x_ref, osb_ref, onsa_ref, mod_ref, gsb_ref, gnsa_ref, w_ref, o_ref):
    a = _rms(osb_ref[...], gsb_ref[...]).astype(MXU_DTYPE)
    b = _rms(onsa_ref[...], gnsa_ref[...]).astype(MXU_DTYPE)
    y = _dot(a, w_ref[:SB_WIDTH, :]) + _dot(b, w_ref[SB_WIDTH:, :])
    o_ref[...] = x_ref[...] + mod_ref[2:3, :] * y


def _out_projection(x, o_sb, o_nsa, mod_l, g_sb, g_nsa, w_out, tm):
    B, T, D = x.shape
    row = lambda w: pl.BlockSpec((None, tm, w), lambda b, t: (b, t, 0))
    return pl.pallas_call(
        _outproj_kernel,
        out_shape=jax.ShapeDtypeStruct((B, T, D), F32),
        grid=(B, T // tm),
        in_specs=[row(D), row(SB_WIDTH), row(NSA_WIDTH),
                  pl.BlockSpec((None, 6, D), lambda b, t: (b, 0, 0)),
                  pl.BlockSpec((1, SB_WIDTH), lambda b, t: (0, 0)),
                  pl.BlockSpec((1, NSA_WIDTH), lambda b, t: (0, 0)),
                  pl.BlockSpec((D, D), lambda b, t: (0, 0))],
        out_specs=row(D),
        compiler_params=_cparams(("parallel", "parallel")),
        name="out_proj",
    )(x, o_sb, o_nsa, mod_l, g_sb, g_nsa, w_out)


def _ffn_kernel(x_ref, mod_ref, g_ref, win_ref, cw_ref, wdn_ref, fg_ref, o_ref, prev_ref,
                *, tm, fc, final_norm):
    t = pl.program_id(1)

    @pl.when(t == 0)
    def _():
        prev_ref[...] = jnp.zeros_like(prev_ref)

    x = x_ref[...]
    h = _rms(x, g_ref[...]) * (1.0 + mod_ref[4:5, :]) + mod_ref[3:4, :]
    hb = h.astype(MXU_DTYPE)
    rowi = _iota((tm, fc), 0)

    def conv(u, c0):
        pm1 = prev_ref[SUBLANES - 1:SUBLANES, c0:c0 + fc]
        pm2 = prev_ref[SUBLANES - 2:SUBLANES - 1, c0:c0 + fc]
        u1 = jnp.where(rowi == 0, pm1, pltpu.roll(u, 1, 0))
        u2 = jnp.where(rowi == 0, pm2, jnp.where(rowi == 1, pm1, pltpu.roll(u, 2, 0)))
        prev_ref[:, c0:c0 + fc] = u[tm - SUBLANES:, :]
        cw = cw_ref[:, c0:c0 + fc]
        return cw[2:3, :] * u + cw[1:2, :] * u1 + cw[0:1, :] * u2 + cw[3:4, :]

    acc = jnp.zeros((tm, D_MODEL), F32)
    for c in range(D_FF // fc):
        ca = c * fc
        cb = D_FF + c * fc
        ya = conv(_dot(hb, win_ref[:, ca:ca + fc]), ca)
        yb = conv(_dot(hb, win_ref[:, cb:cb + fc]), cb)
        z = (ya * jax.nn.sigmoid(ya)) * yb
        acc = acc + _dot(z.astype(MXU_DTYPE), wdn_ref[ca:ca + fc, :])
    y = x + mod_ref[5:6, :] * acc
    if final_norm:
        y = _rms(y, fg_ref[...])
    o_ref[...] = y


def _ffn(x, mod_l, ln_g, w_in, conv_wb, w_down, final_g, tm, fc, final_norm):
    B, T, D = x.shape
    row = pl.BlockSpec((None, tm, D), lambda b, t: (b, t, 0))
    const = lambda r, c: pl.BlockSpec((r, c), lambda b, t: (0, 0), pipeline_mode=pl.Buffered(1))
    kern = functools.partial(_ffn_kernel, tm=tm, fc=fc, final_norm=final_norm)
    return pl.pallas_call(
        kern,
        out_shape=jax.ShapeDtypeStruct((B, T, D), F32),
        grid=(B, T // tm),
        in_specs=[row,
                  pl.BlockSpec((None, 6, D), lambda b, t: (b, 0, 0)),
                  pl.BlockSpec((1, D), lambda b, t: (0, 0)),
                  const(D, 2 * D_FF), const(SUBLANES, 2 * D_FF), const(D_FF, D),
                  pl.BlockSpec((1, D), lambda b, t: (0, 0))],
        out_specs=row,
        scratch_shapes=[pltpu.VMEM((SUBLANES, 2 * D_FF), F32)],
        compiler_params=_cparams(("parallel", "arbitrary")),
        name="conv_ffn",
    )(x, mod_l, ln_g, w_in, conv_wb, w_down, final_g)


_NSA_HEAD_ORDER = [h for p in range(NSA_GROUP) for h in (p, NSA_GROUP + p)]


def _head_cols(order):
    return jnp.asarray([h * HEAD_DIM + d for h in order for d in range(HEAD_DIM)], jnp.int32)


def _prep_w_in(w):
    nq0 = 3 * SB_WIDTH
    kv0 = nq0 + NSA_WIDTH
    gl0 = kv0 + 2 * N_BRANCH * KV_WIDTH
    nq = w[:, nq0:kv0][:, _head_cols(_NSA_HEAD_ORDER)]
    gl = jnp.pad(w[:, gl0:], ((0, 0), (0, LANES - (w.shape[1] - gl0))))
    return jnp.concatenate([w[:, :nq0], nq, w[:, kv0:gl0], gl], axis=1).astype(MXU_DTYPE)


def _tiles(T):
    tm = 256 if T % 256 == 0 else 128
    return tm


def kernel(x, c, ln1_g, ln2_g, w_ada, b_ada, w_in, cmp_pos_k, cmp_w1_k, cmp_w2_k, cmp_pos_v, cmp_w1_v, cmp_w2_v, sb_out_g, nsa_out_g, w_out, ffn_w_in, ffn_conv_w, ffn_conv_b, ffn_w_down, final_g):
    B, T, D = x.shape
    L = w_in.shape[0]
    assert D == D_MODEL and T % 256 == 0 and T // SEL_LEN <= LANES
    tm = 256
    tq = 128
    nch = T // CMP_STRIDE

    half = HEAD_DIM // 2
    inv = ROPE_THETA ** (-jnp.arange(half, dtype=F32) / half)
    inv_lanes = jnp.tile(inv, LANES // half).reshape(1, LANES)
    cos_t, sin_t = _rope_tables(inv_lanes, T, 1, 0)
    cos_c, sin_c = _rope_tables(inv_lanes, nch, CMP_STRIDE, CMP_LEN - 1)

    mod = _modulation(c, w_ada, b_ada).reshape(L, B, 6, D)
    nsa_cols = _head_cols(_NSA_HEAD_ORDER)

    for l in range(L):
        mod_l = mod[l]
        (sbq, sbk, sbv, nq, kc, vc, ks, vs, kw, vw, gates) = _projection(
            x, mod_l, ln1_g[l].reshape(1, D), _prep_w_in(w_in[l]), cos_t, sin_t, tm)
        o_sb = _sb_attention(sbq, sbk, sbv, tq)
        kcmp, vcmp = _compress(kc, vc, cmp_w1_k[l], cmp_w2_k[l], cmp_pos_k[l],
                               cmp_w1_v[l], cmp_w2_v[l], cmp_pos_v[l], cos_c, sin_c)
        o_nsa = _nsa_attention(nq, gates, kcmp, vcmp, ks, vs, kw, vw, tq)
        w_o = jnp.concatenate([w_out[l][:SB_WIDTH], w_out[l][SB_WIDTH:][nsa_cols]], axis=0).astype(MXU_DTYPE)
        x = _out_projection(x, o_sb, o_nsa, mod_l, sb_out_g[l].reshape(1, SB_WIDTH),
                            nsa_out_g[l][nsa_cols].reshape(1, NSA_WIDTH), w_o, tm)
        conv_wb = jnp.concatenate([ffn_conv_w[l], ffn_conv_b[l][None, :],
                                   jnp.zeros((SUBLANES - CONV_W - 1, 2 * D_FF), F32)], axis=0)
        x = _ffn(x, mod_l, ln2_g[l].reshape(1, D), ffn_w_in[l].astype(MXU_DTYPE), conv_wb,
                 ffn_w_down[l].astype(MXU_DTYPE), final_g.reshape(1, D), tm, 256, l == L - 1)
    return x
```

```python
import functools
import math

import jax
import jax.numpy as jnp
from jax import lax
from jax.experimental import pallas as pl
from jax.experimental.pallas import tpu as pltpu

D_MODEL = 1024
HEAD_DIM = 64
SB_HEADS = 8
NSA_HEADS = 8
NSA_KV_HEADS = 2
NSA_GROUP = NSA_HEADS // NSA_KV_HEADS
SB_WIDTH = SB_HEADS * HEAD_DIM
NSA_WIDTH = NSA_HEADS * HEAD_DIM
KV_WIDTH = NSA_KV_HEADS * HEAD_DIM
N_BRANCH = 3
CMP_LEN = 32
CMP_STRIDE = 16
CMP_HIDDEN = 256
SEL_LEN = 64
SEL_TOPK = 16
N_LOCAL = 2
WINDOW = 512
ROPE_THETA = 10000.0
D_FF = 2816
CONV_W = 3
EPS = 1e-6
NEG = -1e30
FORCED_SCORE = 1e6

F32 = jnp.float32
MXU_DTYPE = jnp.bfloat16
LANES = 128
SUBLANES = 8
VMEM_LIMIT = 56 * 1024 * 1024
QK_SCALE = HEAD_DIM ** -0.5
EXP_ZERO_BELOW = -104.0

_NT = (((1,), (1,)), ((), ()))


def _cparams(sem):
    return pltpu.CompilerParams(dimension_semantics=sem, vmem_limit_bytes=VMEM_LIMIT)


def _dot(a, b):
    return jnp.dot(a, b, preferred_element_type=F32)


def _dot_nt(a, b):
    return lax.dot_general(a, b, _NT, preferred_element_type=F32)


def _split_hi_lo(a):
    hi = a.astype(MXU_DTYPE)
    lo = (a - hi.astype(F32)).astype(MXU_DTYPE)
    return hi, lo


def _iota(shape, dim):
    return lax.broadcasted_iota(jnp.int32, shape, dim)


def _mod_kernel(c_ref, w_ref, b_ref, o_ref):
    c = c_ref[...]
    a = c * jax.nn.sigmoid(c)
    o_ref[...] = jnp.dot(a, w_ref[...], preferred_element_type=F32,
                         precision=lax.Precision.HIGHEST) + b_ref[...]


def _modulation(c, w_ada, b_ada):
    L, D, D6 = w_ada.shape
    B = c.shape[0]
    n = D6 // D
    return pl.pallas_call(
        _mod_kernel,
        out_shape=jax.ShapeDtypeStruct((L, B, D6), F32),
        grid=(L, n),
        in_specs=[pl.BlockSpec((B, D), lambda l, j: (0, 0)),
                  pl.BlockSpec((None, D, D), lambda l, j: (l, 0, j)),
                  pl.BlockSpec((None, 1, D), lambda l, j: (l, 0, j))],
        out_specs=pl.BlockSpec((None, B, D), lambda l, j: (l, 0, j)),
        compiler_params=_cparams(("parallel", "parallel")),
        name="adaln_mod",
    )(c, w_ada, b_ada.reshape(L, 1, D6))


def _rope_table_kernel(inv_ref, cos_ref, sin_ref, *, rows, stride, offset):
    i = pl.program_id(0)
    r = _iota((rows, LANES), 0) + i * rows
    pos = (r * stride + offset).astype(F32)
    ang = pos * inv_ref[...]
    lane = _iota((rows, LANES), 1)
    first = (lane % HEAD_DIM) < (HEAD_DIM // 2)
    s = jnp.sin(ang)
    cos_ref[...] = jnp.cos(ang)
    sin_ref[...] = jnp.where(first, -s, s)


def _rope_tables(inv_lanes, n_rows, stride, offset):
    rows = min(n_rows, 512)
    kern = functools.partial(_rope_table_kernel, rows=rows, stride=stride, offset=offset)
    return pl.pallas_call(
        kern,
        out_shape=(jax.ShapeDtypeStruct((n_rows, LANES), F32),) * 2,
        grid=(n_rows // rows,),
        in_specs=[pl.BlockSpec((1, LANES), lambda i: (0, 0))],
        out_specs=(pl.BlockSpec((rows, LANES), lambda i: (i, 0)),) * 2,
        compiler_params=_cparams(("parallel",)),
        name="rope_tables",
    )(inv_lanes)


def _apply_rope(v, cos, sin_signed):
    lane = _iota(v.shape, 1)
    first = (lane % HEAD_DIM) < (HEAD_DIM // 2)
    half = HEAD_DIM // 2
    partner = jnp.where(first, pltpu.roll(v, LANES - half, 1), pltpu.roll(v, half, 1))
    return v * cos + partner * sin_signed


_PROJ_COLS = 3 * SB_WIDTH + NSA_WIDTH + 2 * N_BRANCH * KV_WIDTH + LANES


def _proj_kernel(x_ref, mod_ref, g_ref, w_ref, cos_ref, sin_ref,
                 sbq_ref, sbk_ref, sbv_ref, nq_ref, kc_ref, vc_ref,
                 ks_ref, vs_ref, kw_ref, vw_ref, gate_ref):
    x = x_ref[...]
    shift = mod_ref[0:1, :]
    scale = mod_ref[1:2, :]
    ms = jnp.mean(x * x, axis=-1, keepdims=True)
    h = (x * lax.rsqrt(ms + EPS)) * g_ref[...]
    h = h * (1.0 + scale) + shift
    hb = h.astype(MXU_DTYPE)
    cos = cos_ref[...]
    sin = sin_ref[...]

    def mm(lo, width):
        return _dot(hb, w_ref[:, lo:lo + width])

    o = 0
    sbq_ref[...] = (mm(o, SB_WIDTH) * QK_SCALE).astype(sbq_ref.dtype); o += SB_WIDTH
    sbk_ref[...] = mm(o, SB_WIDTH).astype(sbk_ref.dtype); o += SB_WIDTH
    sbv_ref[...] = mm(o, SB_WIDTH).astype(sbv_ref.dtype); o += SB_WIDTH
    nq = mm(o, NSA_WIDTH); o += NSA_WIDTH
    for p in range(NSA_WIDTH // LANES):
        blk = _apply_rope(nq[:, p * LANES:(p + 1) * LANES], cos, sin) * QK_SCALE
        nq_ref[:, p * LANES:(p + 1) * LANES] = blk.astype(nq_ref.dtype)
    rest = mm(o, 2 * N_BRANCH * KV_WIDTH + LANES)
    kc_ref[...] = rest[:, 0 * LANES:1 * LANES].astype(kc_ref.dtype)
    vc_ref[...] = rest[:, 1 * LANES:2 * LANES].astype(vc_ref.dtype)
    ks_ref[...] = _apply_rope(rest[:, 2 * LANES:3 * LANES], cos, sin).astype(ks_ref.dtype)
    kw_ref[...] = _apply_rope(rest[:, 4 * LANES:5 * LANES], cos, sin).astype(kw_ref.dtype)
    vs_ref[...] = rest[:, 3 * LANES:4 * LANES].T.astype(vs_ref.dtype)
    vw_ref[...] = rest[:, 5 * LANES:6 * LANES].T.astype(vw_ref.dtype)
    gate_ref[...] = jax.nn.sigmoid(rest[:, 6 * LANES:7 * LANES]).T


def _projection(x, mod_l, ln_g, w_cat, cos_t, sin_t, tm):
    B, T, D = x.shape
    row = lambda w: pl.BlockSpec((None, tm, w), lambda b, t: (b, t, 0))
    col = pl.BlockSpec((None, LANES, tm), lambda b, t: (b, 0, t))
    shp = lambda w, dt: jax.ShapeDtypeStruct((B, T, w), dt)
    shp_t = lambda dt: jax.ShapeDtypeStruct((B, LANES, T), dt)
    md = MXU_DTYPE
    return pl.pallas_call(
        _proj_kernel,
        out_shape=(shp(SB_WIDTH, md), shp(SB_WIDTH, md), shp(SB_WIDTH, md), shp(NSA_WIDTH, md),
                   shp(LANES, F32), shp(LANES, F32), shp(LANES, md), shp_t(md),
                   shp(LANES, md), shp_t(md), shp_t(F32)),
        grid=(B, T // tm),
        in_specs=[row(D),
                  pl.BlockSpec((None, 6, D), lambda b, t: (b, 0, 0)),
                  pl.BlockSpec((1, D), lambda b, t: (0, 0)),
                  pl.BlockSpec((D, _PROJ_COLS), lambda b, t: (0, 0)),
                  pl.BlockSpec((tm, LANES), lambda b, t: (t, 0)),
                  pl.BlockSpec((tm, LANES), lambda b, t: (t, 0))],
        out_specs=(row(SB_WIDTH), row(SB_WIDTH), row(SB_WIDTH), row(NSA_WIDTH),
                   row(LANES), row(LANES), row(LANES), col, row(LANES), col, col),
        compiler_params=_cparams(("parallel", "parallel")),
        name="norm_proj",
    )(x, mod_l, ln_g, w_cat, cos_t, sin_t)


def _sb_kernel(q_ref, k_ref, v_ref, uo_ref, o_ref, carry_ref, *, tq):
    i = pl.program_id(1)
    n_pairs = q_ref.shape[1] // LANES
    low = _iota((tq, LANES), 1) < HEAD_DIM
    causal = _iota((tq, tq), 1) < _iota((tq, tq), 0)

    def sweep(kb, diag):
        k0 = pl.multiple_of(kb * tq, tq)
        pairs = [slice(p * LANES, (p + 1) * LANES) for p in range(n_pairs)]
        heads = range(2 * n_pairs)
        old_carry = [None if diag else carry_ref[h] for h in heads]
        old_out = [None if diag else o_ref[:, sl] for sl in pairs]
        uo = uo_ref[...]
        zs = []
        for sl in pairs:
            q = q_ref[:, sl]
            kt = k_ref[pl.ds(k0, tq), sl]
            zq = jnp.zeros_like(q)
            zs.append(_dot_nt(jnp.where(low, q, zq), kt))
            zs.append(_dot_nt(jnp.where(low, zq, q), kt))
        lsns, sums = [], []
        for z in zs:
            lsn = -(jnp.maximum(z, 0.0) + jnp.log(1.0 + jnp.exp(-jnp.abs(z))))
            if diag:
                lsn = jnp.where(causal, lsn, 0.0)
            hi, lo = _split_hi_lo(lsn)
            lsns.append(lsn)
            sums.append(_dot(hi, uo) + _dot(lo, uo))
        ws, new_carry = [], []
        for h in heads:
            tail = sums[h][:, :tq]
            total = sums[h][:, tq:]
            if not diag:
                tail = tail + old_carry[h]
                total = total + old_carry[h]
            w = jnp.exp(zs[h] + lsns[h] + tail)
            if diag:
                w = jnp.where(causal, w, 0.0)
            ws.append(w.astype(MXU_DTYPE))
            new_carry.append(total)
        new_out = []
        for p, sl in enumerate(pairs):
            vt = v_ref[pl.ds(k0, tq), sl]
            zv = jnp.zeros_like(vt)
            v2 = jnp.concatenate([jnp.where(low, vt, zv), jnp.where(low, zv, vt)], axis=0)
            upd = _dot(jnp.concatenate(ws[2 * p:2 * p + 2], axis=1), v2)
            new_out.append(upd if diag else old_out[p] + upd)
        worst = new_carry[0]
        for h in heads:
            carry_ref[h] = new_carry[h]
            worst = jnp.maximum(worst, new_carry[h])
        for p, sl in enumerate(pairs):
            o_ref[:, sl] = new_out[p]
        return (jnp.max(worst) > EXP_ZERO_BELOW).astype(jnp.int32)

    def cond(st):
        kb, live = st
        return jnp.logical_and(kb >= 0, live > 0)

    def body(st):
        kb, _ = st
        return kb - 1, sweep(kb, False)

    lax.while_loop(cond, body, (i - 1, sweep(i, True)))


def _sb_attention(sbq, sbk, sbv, tq):
    B, T, W = sbq.shape
    r = jnp.arange(tq)
    u = (r[:, None] > r[None, :]).astype(MXU_DTYPE)
    uo = jnp.concatenate([u, jnp.ones((tq, tq), MXU_DTYPE)], axis=1)
    kern = functools.partial(_sb_kernel, tq=tq)
    return pl.pallas_call(
        kern,
        out_shape=jax.ShapeDtypeStruct((B, T, W), F32),
        grid=(B, T // tq),
        in_specs=[pl.BlockSpec((None, tq, W), lambda b, i: (b, i, 0)),
                  pl.BlockSpec((None, T, W), lambda b, i: (b, 0, 0)),
                  pl.BlockSpec((None, T, W), lambda b, i: (b, 0, 0)),
                  pl.BlockSpec((tq, 2 * tq), lambda b, i: (0, 0))],
        out_specs=pl.BlockSpec((None, tq, W), lambda b, i: (b, i, 0)),
        scratch_shapes=[pltpu.VMEM((W // HEAD_DIM, tq, tq), F32)],
        compiler_params=_cparams(("parallel", "parallel")),
        name="sb_attention",
    )(sbq, sbk, sbv, uo)


def _gelu_tanh(x):
    c = math.sqrt(2.0 / math.pi)
    return 0.5 * x * (1.0 + jnp.tanh(c * (x + 0.044715 * (x * x * x))))


def _compress_kernel(kc_ref, vc_ref, wck_ref, w1k_ref, w2k_ref, pk_ref, wcv_ref, w1v_ref, w2v_ref, pv_ref,
                     cos_ref, sin_ref, ko_ref, vo_ref, *, nch):
    rowi = _iota((nch, LANES), 0)

    def one(c_ref, wc_ref, w1_ref, w2_ref, p_ref, rope):
        bias = _dot(p_ref[...], w1_ref[...])[0:1, :]
        proj = jnp.zeros((nch, 2 * NSA_KV_HEADS * CMP_HIDDEN), F32)
        for l in range(CMP_STRIDE):
            x_l = c_ref[pl.ds(l, nch, stride=CMP_STRIDE), :].astype(MXU_DTYPE)
            proj = proj + _dot(x_l, wc_ref[l])
        acc = jnp.zeros((nch, LANES), F32)
        for g in range(NSA_KV_HEADS):
            a = proj[:, (2 * g) * CMP_HIDDEN:(2 * g + 1) * CMP_HIDDEN]
            b = proj[:, (2 * g + 1) * CMP_HIDDEN:(2 * g + 2) * CMP_HIDDEN]
            hid = a + pltpu.roll(b, nch - 1, 0) + bias
            acc = acc + _dot(_gelu_tanh(hid).astype(MXU_DTYPE), w2_ref[g])
        if rope:
            acc = _apply_rope(acc, cos_ref[...], sin_ref[...])
        return jnp.where(rowi < nch - 1, acc, 0.0)

    ko_ref[...] = one(kc_ref, wck_ref, w1k_ref, w2k_ref, pk_ref, True).astype(ko_ref.dtype)
    vo_ref[...] = one(vc_ref, wcv_ref, w1v_ref, w2v_ref, pv_ref, False).T.astype(vo_ref.dtype)


def _pad_w2(w2):
    z = jnp.zeros_like(w2)
    return jnp.stack([jnp.concatenate([w2, z], 1), jnp.concatenate([z, w2], 1)]).astype(MXU_DTYPE)


def _chunk_weights(w1):
    half = CMP_STRIDE * HEAD_DIM
    top = w1[:half].reshape(CMP_STRIDE, HEAD_DIM, CMP_HIDDEN)
    bot = w1[half:].reshape(CMP_STRIDE, HEAD_DIM, CMP_HIDDEN)
    tb = jnp.concatenate([top, bot], axis=2)
    z = jnp.zeros_like(tb)
    return jnp.concatenate([jnp.concatenate([tb, z], axis=2),
                            jnp.concatenate([z, tb], axis=2)], axis=1).astype(MXU_DTYPE)


def _compress(kc, vc, w1k, w2k, pk, w1v, w2v, pv, cos_c, sin_c):
    B, T, _ = kc.shape
    nch = T // CMP_STRIDE
    flat = CMP_STRIDE * HEAD_DIM
    posf = lambda p: jnp.broadcast_to(p.reshape(1, CMP_LEN * HEAD_DIM), (SUBLANES, CMP_LEN * HEAD_DIM)).astype(MXU_DTYPE)
    full = lambda *s: pl.BlockSpec(s, lambda b: (0,) * len(s))
    wide = 2 * NSA_KV_HEADS * CMP_HIDDEN
    kern = functools.partial(_compress_kernel, nch=nch)
    return pl.pallas_call(
        kern,
        out_shape=(jax.ShapeDtypeStruct((B, nch, LANES), MXU_DTYPE),
                   jax.ShapeDtypeStruct((B, LANES, nch), MXU_DTYPE)),
        grid=(B,),
        in_specs=[pl.BlockSpec((None, T, LANES), lambda b: (b, 0, 0)),
                  pl.BlockSpec((None, T, LANES), lambda b: (b, 0, 0)),
                  full(CMP_STRIDE, LANES, wide), full(2 * flat, CMP_HIDDEN), full(2, CMP_HIDDEN, LANES),
                  full(SUBLANES, 2 * flat),
                  full(CMP_STRIDE, LANES, wide), full(2 * flat, CMP_HIDDEN), full(2, CMP_HIDDEN, LANES),
                  full(SUBLANES, 2 * flat),
                  full(nch, LANES), full(nch, LANES)],
        out_specs=(pl.BlockSpec((None, nch, LANES), lambda b: (b, 0, 0)),
                   pl.BlockSpec((None, LANES, nch), lambda b: (b, 0, 0))),
        compiler_params=_cparams(("parallel",)),
        name="nsa_compress",
    )(kc, vc, _chunk_weights(w1k), w1k.astype(MXU_DTYPE), _pad_w2(w2k), posf(pk),
      _chunk_weights(w1v), w1v.astype(MXU_DTYPE), _pad_w2(w2v), posf(pv), cos_c, sin_c)


def _nsa_kernel(q_ref, gt_ref, kcmp_ref, vcmpt_ref, ks_ref, vst_ref, kw_ref, vwt_ref,
                ovt_ref, eselt_ref, o_ref, qa_ref, s_ref, *, tq, n_sel, nsp):
    i = pl.program_id(1)
    q0 = i * tq
    ncmp = kcmp_ref.shape[0]
    top = min(SEL_TOPK, n_sel)
    wide = 2 * tq
    heads = range(NSA_HEADS)
    lane = _iota((tq, LANES), 1)

    for g in range(NSA_KV_HEADS):
        in_group = (lane // HEAD_DIM) == g
        for p in range(NSA_GROUP):
            blk = q_ref[:, p * LANES:(p + 1) * LANES]
            qa_ref[g, p * tq:(p + 1) * tq, :LANES] = jnp.where(in_group, blk, jnp.zeros_like(blk))

    def fresh_state():
        return [(jnp.full((1, tq), NEG, F32), jnp.zeros((HEAD_DIM + SUBLANES, tq), F32)) for _ in heads]

    def flash(state, s_ts, bias, vt_ref, k0, w):
        new = []
        for g in range(NSA_KV_HEADS):
            vt = jnp.concatenate([vt_ref[g * HEAD_DIM:(g + 1) * HEAD_DIM, pl.ds(k0, w)],
                                  jnp.ones((SUBLANES, w), MXU_DTYPE)], axis=0)
            for hl in range(NSA_GROUP):
                m_old, acc_old = state[g * NSA_GROUP + hl]
                x = s_ts[g][:, hl * tq:(hl + 1) * tq]
                if bias is not None:
                    x = x + bias
                m_new = jnp.maximum(m_old, jnp.max(x, axis=0, keepdims=True))
                p = jnp.exp(x - m_new).astype(MXU_DTYPE)
                new.append((m_new, jnp.exp(m_old - m_new) * acc_old + _dot(vt, p)))
        return new

    def gated(state, branch):
        res = []
        for h in heads:
            acc = state[h][1]
            r = N_BRANCH * h + branch
            scale = gt_ref[r:r + 1, :] * (1.0 / jnp.maximum(acc[HEAD_DIM:HEAD_DIM + 1, :], 1e-6))
            res.append(acc[:HEAD_DIM, :] * scale)
        return res

    kcmp = kcmp_ref[...]
    valid_c = (_iota((ncmp, tq), 0) * CMP_STRIDE + (CMP_LEN - 1)) <= (q0 + _iota((ncmp, tq), 1))
    bias_c = jnp.where(valid_c, 0.0, NEG)
    keep_c = jnp.where(valid_c, 1.0, 0.0)
    n_chunks = nsp // SUBLANES
    jrow = _iota((SUBLANES, tq), 0)
    qblk = (q0 + _iota((SUBLANES, tq), 1)) // SEL_LEN
    outs = [None] * NSA_HEADS
    for g in range(NSA_KV_HEADS):
        s_t = _dot_nt(kcmp, qa_ref[g, :, :LANES])
        vct = vcmpt_ref[g * HEAD_DIM:(g + 1) * HEAD_DIM, :]
        psum = None
        for hl in range(NSA_GROUP):
            h = g * NSA_GROUP + hl
            x = s_t[:, hl * tq:(hl + 1) * tq] + bias_c
            p = jnp.exp(x - jnp.max(x, axis=0, keepdims=True)) * keep_c
            p = p * (1.0 / jnp.maximum(jnp.sum(p, axis=0, keepdims=True), 1e-6))
            psum = p if psum is None else psum + p
            outs[h] = gt_ref[N_BRANCH * h:N_BRANCH * h + 1, :] * _dot(vct, p.astype(MXU_DTYPE))
        ph, plo = _split_hi_lo(psum)
        ovt = ovt_ref[...]
        imp_t = _dot(ovt, ph) + _dot(ovt, plo)
        scores = []
        for c in range(n_chunks):
            dist = qblk - (jrow + c * SUBLANES)
            forced = (dist >= 0) & (dist < N_LOCAL)
            if c == 0:
                forced = forced | (jrow == 0)
            imp_c = imp_t[c * SUBLANES:(c + 1) * SUBLANES, :]
            scores.append(jnp.where(dist < 0, -1.0, jnp.where(forced, FORCED_SCORE, imp_c)))
        ranks = [jnp.zeros((SUBLANES, tq), F32) for _ in range(n_chunks)]
        for i2 in range(n_sel):
            c2, r2 = divmod(i2, SUBLANES)
            r = jnp.broadcast_to(scores[c2][r2:r2 + 1, :], (SUBLANES, tq))
            for c in range(n_chunks):
                if c < c2:
                    beats = r > scores[c]
                elif c > c2:
                    beats = r >= scores[c]
                else:
                    beats = (r > scores[c]) | ((r == scores[c]) & (jrow > r2))
                ranks[c] = ranks[c] + jnp.where(beats, 1.0, 0.0)
        nsel = [jnp.where((ranks[c] < top) & (scores[c] >= 0.0), 0.0, NEG) for c in range(n_chunks)]
        if nsp < LANES:
            nsel.append(jnp.zeros((LANES - nsp, tq), F32))
        nsel_q = jnp.concatenate(nsel, axis=0).T.astype(MXU_DTYPE)
        for p in range(NSA_GROUP):
            qa_ref[g, p * tq:(p + 1) * tq, LANES:] = nsel_q

    t_max = eselt_ref.shape[0] // wide - 1

    def tile_start(t):
        return pl.multiple_of(jnp.minimum(t, t_max) * wide, wide)

    def sel_scores(t, slot):
        k0 = tile_start(t)
        kaug = jnp.concatenate([ks_ref[pl.ds(k0, wide), :], eselt_ref[pl.ds(k0, wide), :]], axis=1)
        for g in range(NSA_KV_HEADS):
            s_ref[slot, g] = _dot_nt(kaug, qa_ref[g])

    def sel_flash(state, t, slot, causal):
        bias = None
        if causal:
            bias = jnp.where(_iota((wide, tq), 0) - _iota((wide, tq), 1) <= q0 - t * wide, 0.0, NEG)
        s_ts = [s_ref[slot, g] for g in range(NSA_KV_HEADS)]
        return flash(state, s_ts, bias, vst_ref, tile_start(t), wide)

    def sel_body(u, state):
        sel_scores(2 * u + 1, 1)
        state = sel_flash(state, 2 * u, 0, False)
        sel_scores(2 * u + 2, 0)
        return sel_flash(state, 2 * u + 1, 1, False)

    n_tiles = i // 2 + 1
    n_even = n_tiles + n_tiles % 2
    sel_scores(0, 0)
    sel_state = lax.fori_loop(0, n_even // 2 - 1, sel_body, fresh_state())
    sel_scores(n_even - 1, 1)
    sel_state = sel_flash(sel_state, n_even - 2, 0, True)

    n_win = WINDOW // tq

    def win_scores(j):
        k0 = pl.multiple_of(jnp.maximum(q0 - (n_win - j) * tq, 0), tq)
        kt = kw_ref[pl.ds(k0, tq), :]
        return [_dot_nt(kt, qa_ref[g, :, :LANES]) for g in range(NSA_KV_HEADS)], k0

    krow = _iota((tq, tq), 0)
    qcol = _iota((tq, tq), 1)
    causal_bias = jnp.where(krow <= qcol, 0.0, NEG)
    oldest_bias = jnp.where(krow > qcol, 0.0, NEG)

    w_scores, w_k0 = win_scores(0)
    sel_state = sel_flash(sel_state, n_even - 1, 1, True)
    for h, res in enumerate(gated(sel_state, 1)):
        outs[h] = outs[h] + res

    win_state = fresh_state()
    for j in range(n_win + 1):
        nxt = win_scores(j + 1) if j < n_win else None
        bias = oldest_bias if j == 0 else (causal_bias if j == n_win else None)
        new_state = flash(win_state, w_scores, bias, vwt_ref, w_k0, tq)
        if j < n_win:
            live = i >= n_win - j
            new_state = [(jnp.where(live, mn, mo), jnp.where(live, an, ao))
                         for (mn, an), (mo, ao) in zip(new_state, win_state)]
            w_scores, w_k0 = nxt
        win_state = new_state
    for h, res in enumerate(gated(win_state, 2)):
        outs[h] = outs[h] + res

    for hl in range(NSA_GROUP):
        pair_t = jnp.concatenate([outs[hl], outs[NSA_GROUP + hl]], axis=0)
        o_ref[:, hl * LANES:(hl + 1) * LANES] = pair_t.T


def _nsa_attention(nq, gates_t, kcmp, vcmp_t, ks, vs_t, kw, vw_t, tq):
    B, T, _ = nq.shape
    ncmp = kcmp.shape[1]
    n_sel = T // SEL_LEN
    nsp = -(-n_sel // SUBLANES) * SUBLANES
    cs = jnp.arange(ncmp) * CMP_STRIDE
    ss = jnp.arange(nsp) * SEL_LEN
    ov = jnp.clip(jnp.minimum(cs[None, :] + CMP_LEN, ss[:, None] + SEL_LEN)
                  - jnp.maximum(cs[None, :], ss[:, None]), 0, None).astype(F32) / CMP_LEN
    ov = jnp.where((jnp.arange(nsp) < n_sel)[:, None] & (jnp.arange(ncmp) < ncmp - 1)[None, :], ov, 0.0)
    ovt = ov.astype(MXU_DTYPE)
    esel_t = ((jnp.arange(T) // SEL_LEN)[:, None] == jnp.arange(LANES)[None, :]).astype(MXU_DTYPE)
    kern = functools.partial(_nsa_kernel, tq=tq, n_sel=n_sel, nsp=nsp)
    per_b = lambda r, w: pl.BlockSpec((None, r, w), lambda b, i: (b, 0, 0))
    return pl.pallas_call(
        kern,
        out_shape=jax.ShapeDtypeStruct((B, T, NSA_WIDTH), F32),
        grid=(B, T // tq),
        in_specs=[pl.BlockSpec((None, tq, NSA_WIDTH), lambda b, i: (b, i, 0)),
                  pl.BlockSpec((None, LANES, tq), lambda b, i: (b, 0, i)),
                  per_b(ncmp, LANES), per_b(LANES, ncmp),
                  per_b(T, LANES), per_b(LANES, T), per_b(T, LANES), per_b(LANES, T),
                  pl.BlockSpec((nsp, ncmp), lambda b, i: (0, 0)),
                  pl.BlockSpec((T, LANES), lambda b, i: (0, 0))],
        out_specs=pl.BlockSpec((None, tq, NSA_WIDTH), lambda b, i: (b, i, 0)),
        scratch_shapes=[pltpu.VMEM((NSA_KV_HEADS, NSA_GROUP * tq, 2 * LANES), MXU_DTYPE),
                        pltpu.VMEM((2, NSA_KV_HEADS, 2 * tq, NSA_GROUP * tq), F32)],
        compiler_params=_cparams(("parallel", "parallel")),
        name="nsa_attention",
    )(nq, gates_t, kcmp, vcmp_t, ks, vs_t, kw, vw_t, ovt, esel_t)


def _rms(v, g):
    return (v * lax.rsqrt(jnp.mean(v * v, axis=-1, keepdims=True) + EPS)) * g


def _outproj_kernel(x_ref, osb_ref, onsa_ref, mod_ref, gsb_ref, gnsa_ref, w_ref, o_ref):
    a = _rms(osb_ref[...], gsb_ref[...]).astype(MXU_DTYPE)
    b = _rms(onsa_ref[...], gnsa_ref[...]).astype(MXU_DTYPE)
    y = _dot(a, w_ref[:SB_WIDTH, :]) + _dot(b, w_ref[SB_WIDTH:, :])
    o_ref[...] = x_ref[...] + mod_ref[2:3, :] * y


def _out_projection(x, o_sb, o_nsa, mod_l, g_sb, g_nsa, w_out, tm):
    B, T, D = x.shape
    row = lambda w: pl.BlockSpec((None, tm, w), lambda b, t: (b, t, 0))
    return pl.pallas_call(
        _outproj_kernel,
        out_shape=jax.ShapeDtypeStruct((B, T, D), F32),
        grid=(B, T // tm),
        in_specs=[row(D), row(SB_WIDTH), row(NSA_WIDTH),
                  pl.BlockSpec((None, 6, D), lambda b, t: (b, 0, 0)),
                  pl.BlockSpec((1, SB_WIDTH), lambda b, t: (0, 0)),
                  pl.BlockSpec((1, NSA_WIDTH), lambda b, t: (0, 0)),
                  pl.BlockSpec((D, D), lambda b, t: (0, 0))],
        out_specs=row(D),
        compiler_params=_cparams(("parallel", "parallel")),
        name="out_proj",
    )(x, o_sb, o_nsa, mod_l, g_sb, g_nsa, w_out)


def _ffn_kernel(x_ref, mod_ref, g_ref, win_ref, cw_ref, wdn_ref, fg_ref, o_ref, prev_ref, ubuf_ref,
                *, tm, fc, final_norm):
    t = pl.program_id(1)

    @pl.when(t == 0)
    def _():
        prev_ref[...] = jnp.zeros_like(prev_ref)

    x = x_ref[...]
    h = _rms(x, g_ref[...]) * (1.0 + mod_ref[4:5, :]) + mod_ref[3:4, :]
    hb = h.astype(MXU_DTYPE)
    def conv(u, c0, slot):
        ubuf_ref[slot, :SUBLANES, :] = prev_ref[:, c0:c0 + fc]
        ubuf_ref[slot, SUBLANES:, :] = u
        prev_ref[:, c0:c0 + fc] = u[tm - SUBLANES:, :]
        u1 = ubuf_ref[slot, SUBLANES - 1:SUBLANES - 1 + tm, :]
        u2 = ubuf_ref[slot, SUBLANES - 2:SUBLANES - 2 + tm, :]
        cw = cw_ref[:, c0:c0 + fc]
        return cw[2:3, :] * u + cw[1:2, :] * u1 + cw[0:1, :] * u2 + cw[3:4, :]

    def up(c):
        return (_dot(hb, win_ref[:, c * fc:(c + 1) * fc]),
                _dot(hb, win_ref[:, D_FF + c * fc:D_FF + (c + 1) * fc]))

    n_chunks = D_FF // fc
    acc = jnp.zeros((tm, D_MODEL), F32)
    ua, ub = up(0)
    for c in range(n_chunks):
        nxt = up(c + 1) if c + 1 < n_chunks else None
        ya = conv(ua, c * fc, 2 * (c % 2))
        yb = conv(ub, D_FF + c * fc, 2 * (c % 2) + 1)
        z = (ya * jax.nn.sigmoid(ya)) * yb
        acc = acc + _dot(z.astype(MXU_DTYPE), wdn_ref[c * fc:(c + 1) * fc, :])
        if nxt is not None:
            ua, ub = nxt
    y = x + mod_ref[5:6, :] * acc
    if final_norm:
        y = _rms(y, fg_ref[...])
    o_ref[...] = y


def _ffn(x, mod_l, ln_g, w_in, conv_wb, w_down, final_g, tm, fc, final_norm):
    B, T, D = x.shape
    row = pl.BlockSpec((None, tm, D), lambda b, t: (b, t, 0))
    const = lambda r, c: pl.BlockSpec((r, c), lambda b, t: (0, 0), pipeline_mode=pl.Buffered(1))
    kern = functools.partial(_ffn_kernel, tm=tm, fc=fc, final_norm=final_norm)
    return pl.pallas_call(
        kern,
        out_shape=jax.ShapeDtypeStruct((B, T, D), F32),
        grid=(B, T // tm),
        in_specs=[row,
                  pl.BlockSpec((None, 6, D), lambda b, t: (b, 0, 0)),
                  pl.BlockSpec((1, D), lambda b, t: (0, 0)),
                  const(D, 2 * D_FF), const(SUBLANES, 2 * D_FF), const(D_FF, D),
                  pl.BlockSpec((1, D), lambda b, t: (0, 0))],
        out_specs=row,
        scratch_shapes=[pltpu.VMEM((SUBLANES, 2 * D_FF), F32),
                        pltpu.VMEM((4, SUBLANES + tm, fc), F32)],
        compiler_params=_cparams(("parallel", "arbitrary")),
        name="conv_ffn",
    )(x, mod_l, ln_g, w_in, conv_wb, w_down, final_g)


_NSA_HEAD_ORDER = [h for p in range(NSA_GROUP) for h in (p, NSA_GROUP + p)]


def _head_cols(order):
    return jnp.asarray([h * HEAD_DIM + d for h in order for d in range(HEAD_DIM)], jnp.int32)


def _prep_w_in(w):
    nq0 = 3 * SB_WIDTH
    kv0 = nq0 + NSA_WIDTH
    gl0 = kv0 + 2 * N_BRANCH * KV_WIDTH
    nq = w[:, nq0:kv0][:, _head_cols(_NSA_HEAD_ORDER)]
    gl = jnp.pad(w[:, gl0:], ((0, 0), (0, LANES - (w.shape[1] - gl0))))
    return jnp.concatenate([w[:, :nq0], nq, w[:, kv0:gl0], gl], axis=1).astype(MXU_DTYPE)


def _tiles(T):
    tm = 256 if T % 256 == 0 else 128
    return tm


def kernel(x, c, ln1_g, ln2_g, w_ada, b_ada, w_in, cmp_pos_k, cmp_w1_k, cmp_w2_k, cmp_pos_v, cmp_w1_v, cmp_w2_v, sb_out_g, nsa_out_g, w_out, ffn_w_in, ffn_conv_w, ffn_conv_b, ffn_w_down, final_g):
    B, T, D = x.shape
    L = w_in.shape[0]
    assert D == D_MODEL and T % 256 == 0 and T // SEL_LEN <= LANES
    tm = 256
    tq = 128
    nch = T // CMP_STRIDE

    half = HEAD_DIM // 2
    inv = ROPE_THETA ** (-jnp.arange(half, dtype=F32) / half)
    inv_lanes = jnp.tile(inv, LANES // half).reshape(1, LANES)
    cos_t, sin_t = _rope_tables(inv_lanes, T, 1, 0)
    cos_c, sin_c = _rope_tables(inv_lanes, nch, CMP_STRIDE, CMP_LEN - 1)

    mod = _modulation(c, w_ada, b_ada).reshape(L, B, 6, D)
    nsa_cols = _head_cols(_NSA_HEAD_ORDER)

    for l in range(L):
        mod_l = mod[l]
        (sbq, sbk, sbv, nq, kc, vc, ks, vs, kw, vw, gates) = _projection(
            x, mod_l, ln1_g[l].reshape(1, D), _prep_w_in(w_in[l]), cos_t, sin_t, tm)
        o_sb = _sb_attention(sbq, sbk, sbv, tq)
        kcmp, vcmp = _compress(kc, vc, cmp_w1_k[l], cmp_w2_k[l], cmp_pos_k[l],
                               cmp_w1_v[l], cmp_w2_v[l], cmp_pos_v[l], cos_c, sin_c)
        o_nsa = _nsa_attention(nq, gates, kcmp, vcmp, ks, vs, kw, vw, tq)
        w_o = jnp.concatenate([w_out[l][:SB_WIDTH], w_out[l][SB_WIDTH:][nsa_cols]], axis=0).astype(MXU_DTYPE)
        x = _out_projection(x, o_sb, o_nsa, mod_l, sb_out_g[l].reshape(1, SB_WIDTH),
                            nsa_out_g[l][nsa_cols].reshape(1, NSA_WIDTH), w_o, tm)
        conv_wb = jnp.concatenate([ffn_conv_w[l], ffn_conv_b[l][None, :],
                                   jnp.zeros((SUBLANES - CONV_W - 1, 2 * D_FF), F32)], axis=0)
        x = _ffn(x, mod_l, ln2_g[l].reshape(1, D), ffn_w_in[l].astype(MXU_DTYPE), conv_wb,
                 ffn_w_down[l].astype(MXU_DTYPE), final_g.reshape(1, D), tm, 256, l == L - 1)
    return x
```

```python
import functools
import math

import jax
import jax.numpy as jnp
from jax import lax
from jax.experimental import pallas as pl
from jax.experimental.pallas import tpu as pltpu

D_MODEL = 1024
HEAD_DIM = 64
SB_HEADS = 8
NSA_HEADS = 8
NSA_KV_HEADS = 2
NSA_GROUP = NSA_HEADS // NSA_KV_HEADS
SB_WIDTH = SB_HEADS * HEAD_DIM
NSA_WIDTH = NSA_HEADS * HEAD_DIM
KV_WIDTH = NSA_KV_HEADS * HEAD_DIM
N_BRANCH = 3
CMP_LEN = 32
CMP_STRIDE = 16
CMP_HIDDEN = 256
SEL_LEN = 64
SEL_TOPK = 16
N_LOCAL = 2
WINDOW = 512
ROPE_THETA = 10000.0
D_FF = 2816
CONV_W = 3
EPS = 1e-6
NEG = -1e30
FORCED_SCORE = 1e6

F32 = jnp.float32
MXU_DTYPE = jnp.bfloat16
LANES = 128
SUBLANES = 8
VMEM_LIMIT = 56 * 1024 * 1024
QK_SCALE = HEAD_DIM ** -0.5
EXP_ZERO_BELOW = -104.0

_NT = (((1,), (1,)), ((), ()))


def _cparams(sem):
    return pltpu.CompilerParams(dimension_semantics=sem, vmem_limit_bytes=VMEM_LIMIT)


def _dot(a, b):
    return jnp.dot(a, b, preferred_element_type=F32)


def _dot_nt(a, b):
    return lax.dot_general(a, b, _NT, preferred_element_type=F32)


def _split_hi_lo(a):
    hi = a.astype(MXU_DTYPE)
    lo = (a - hi.astype(F32)).astype(MXU_DTYPE)
    return hi, lo


def _iota(shape, dim):
    return lax.broadcasted_iota(jnp.int32, shape, dim)


def _mod_kernel(c_ref, w_ref, b_ref, o_ref):
    c = c_ref[...]
    a = c * jax.nn.sigmoid(c)
    o_ref[...] = jnp.dot(a, w_ref[...], preferred_element_type=F32,
                         precision=lax.Precision.HIGHEST) + b_ref[...]


def _modulation(c, w_ada, b_ada):
    L, D, D6 = w_ada.shape
    B = c.shape[0]
    n = D6 // D
    return pl.pallas_call(
        _mod_kernel,
        out_shape=jax.ShapeDtypeStruct((L, B, D6), F32),
        grid=(L, n),
        in_specs=[pl.BlockSpec((B, D), lambda l, j: (0, 0)),
                  pl.BlockSpec((None, D, D), lambda l, j: (l, 0, j)),
                  pl.BlockSpec((None, 1, D), lambda l, j: (l, 0, j))],
        out_specs=pl.BlockSpec((None, B, D), lambda l, j: (l, 0, j)),
        compiler_params=_cparams(("parallel", "parallel")),
        name="adaln_mod",
    )(c, w_ada, b_ada.reshape(L, 1, D6))


def _rope_table_kernel(inv_ref, cos_ref, sin_ref, *, rows, stride, offset):
    i = pl.program_id(0)
    r = _iota((rows, LANES), 0) + i * rows
    pos = (r * stride + offset).astype(F32)
    ang = pos * inv_ref[...]
    lane = _iota((rows, LANES), 1)
    first = (lane % HEAD_DIM) < (HEAD_DIM // 2)
    s = jnp.sin(ang)
    cos_ref[...] = jnp.cos(ang)
    sin_ref[...] = jnp.where(first, -s, s)


def _rope_tables(inv_lanes, n_rows, stride, offset):
    rows = min(n_rows, 512)
    kern = functools.partial(_rope_table_kernel, rows=rows, stride=stride, offset=offset)
    return pl.pallas_call(
        kern,
        out_shape=(jax.ShapeDtypeStruct((n_rows, LANES), F32),) * 2,
        grid=(n_rows // rows,),
        in_specs=[pl.BlockSpec((1, LANES), lambda i: (0, 0))],
        out_specs=(pl.BlockSpec((rows, LANES), lambda i: (i, 0)),) * 2,
        compiler_params=_cparams(("parallel",)),
        name="rope_tables",
    )(inv_lanes)


def _apply_rope(v, cos, sin_signed):
    lane = _iota(v.shape, 1)
    first = (lane % HEAD_DIM) < (HEAD_DIM // 2)
    half = HEAD_DIM // 2
    partner = jnp.where(first, pltpu.roll(v, LANES - half, 1), pltpu.roll(v, half, 1))
    return v * cos + partner * sin_signed


_PROJ_COLS = 3 * SB_WIDTH + NSA_WIDTH + 2 * N_BRANCH * KV_WIDTH + LANES


def _proj_kernel(x_ref, mod_ref, g_ref, w_ref, cos_ref, sin_ref,
                 sbq_ref, sbk_ref, sbv_ref, nq_ref, kc_ref, vc_ref,
                 ks_ref, vs_ref, kw_ref, vw_ref, gate_ref):
    x = x_ref[...]
    shift = mod_ref[0:1, :]
    scale = mod_ref[1:2, :]
    ms = jnp.mean(x * x, axis=-1, keepdims=True)
    h = (x * lax.rsqrt(ms + EPS)) * g_ref[...]
    h = h * (1.0 + scale) + shift
    hb = h.astype(MXU_DTYPE)
    cos = cos_ref[...]
    sin = sin_ref[...]

    def mm(lo, width):
        return _dot(hb, w_ref[:, lo:lo + width])

    o = 0
    sbq_ref[...] = (mm(o, SB_WIDTH) * QK_SCALE).astype(sbq_ref.dtype); o += SB_WIDTH
    sbk_ref[...] = mm(o, SB_WIDTH).astype(sbk_ref.dtype); o += SB_WIDTH
    sbv_ref[...] = mm(o, SB_WIDTH).astype(sbv_ref.dtype); o += SB_WIDTH
    nq = mm(o, NSA_WIDTH); o += NSA_WIDTH
    for p in range(NSA_WIDTH // LANES):
        blk = _apply_rope(nq[:, p * LANES:(p + 1) * LANES], cos, sin) * QK_SCALE
        nq_ref[:, p * LANES:(p + 1) * LANES] = blk.astype(nq_ref.dtype)
    rest = mm(o, 2 * N_BRANCH * KV_WIDTH + LANES)
    kc_ref[...] = rest[:, 0 * LANES:1 * LANES].astype(kc_ref.dtype)
    vc_ref[...] = rest[:, 1 * LANES:2 * LANES].astype(vc_ref.dtype)
    ks_ref[...] = _apply_rope(rest[:, 2 * LANES:3 * LANES], cos, sin).astype(ks_ref.dtype)
    kw_ref[...] = _apply_rope(rest[:, 4 * LANES:5 * LANES], cos, sin).astype(kw_ref.dtype)
    vs_ref[...] = rest[:, 3 * LANES:4 * LANES].T.astype(vs_ref.dtype)
    vw_ref[...] = rest[:, 5 * LANES:6 * LANES].T.astype(vw_ref.dtype)
    gate_ref[...] = jax.nn.sigmoid(rest[:, 6 * LANES:7 * LANES]).T


def _projection(x, mod_l, ln_g, w_cat, cos_t, sin_t, tm):
    B, T, D = x.shape
    row = lambda w: pl.BlockSpec((None, tm, w), lambda b, t: (b, t, 0))
    col = pl.BlockSpec((None, LANES, tm), lambda b, t: (b, 0, t))
    shp = lambda w, dt: jax.ShapeDtypeStruct((B, T, w), dt)
    shp_t = lambda dt: jax.ShapeDtypeStruct((B, LANES, T), dt)
    md = MXU_DTYPE
    return pl.pallas_call(
        _proj_kernel,
        out_shape=(shp(SB_WIDTH, md), shp(SB_WIDTH, md), shp(SB_WIDTH, md), shp(NSA_WIDTH, md),
                   shp(LANES, F32), shp(LANES, F32), shp(LANES, md), shp_t(md),
                   shp(LANES, md), shp_t(md), shp_t(F32)),
        grid=(B, T // tm),
        in_specs=[row(D),
                  pl.BlockSpec((None, 6, D), lambda b, t: (b, 0, 0)),
                  pl.BlockSpec((1, D), lambda b, t: (0, 0)),
                  pl.BlockSpec((D, _PROJ_COLS), lambda b, t: (0, 0)),
                  pl.BlockSpec((tm, LANES), lambda b, t: (t, 0)),
                  pl.BlockSpec((tm, LANES), lambda b, t: (t, 0))],
        out_specs=(row(SB_WIDTH), row(SB_WIDTH), row(SB_WIDTH), row(NSA_WIDTH),
                   row(LANES), row(LANES), row(LANES), col, row(LANES), col, col),
        compiler_params=_cparams(("parallel", "parallel")),
        name="norm_proj",
    )(x, mod_l, ln_g, w_cat, cos_t, sin_t)


def _sb_kernel(q_ref, k_ref, v_ref, uo_ref, o_ref, carry_ref, *, tq):
    i = pl.program_id(1)
    n_pairs = q_ref.shape[1] // LANES
    low = _iota((tq, LANES), 1) < HEAD_DIM
    causal = _iota((tq, tq), 1) < _iota((tq, tq), 0)

    def sweep(tiles):
        pairs = [slice(p * LANES, (p + 1) * LANES) for p in range(n_pairs)]
        heads = range(2 * n_pairs)
        resume = tiles[0][1] == "back"
        carry = [carry_ref[h] if resume else None for h in heads]
        out = [o_ref[:, sl] if resume else None for sl in pairs]
        exists = [kb >= 0 for kb, _ in tiles]
        uo = uo_ref[...]
        starts = [pl.multiple_of(jnp.maximum(kb, 0) * tq, tq) for kb, _ in tiles]
        zs = []
        for k0 in starts:
            for sl in pairs:
                q = q_ref[:, sl]
                kt = k_ref[pl.ds(k0, tq), sl]
                zq = jnp.zeros_like(q)
                zs.append(_dot_nt(jnp.where(low, q, zq), kt))
                zs.append(_dot_nt(jnp.where(low, zq, q), kt))
        lsns, sums = [], []
        for n, z in enumerate(zs):
            t = n // len(heads)
            mode = tiles[t][1]
            lsn = -(jnp.maximum(z, 0.0) + jnp.log(1.0 + jnp.exp(-jnp.abs(z))))
            if mode == "diag":
                lsn = jnp.where(causal, lsn, 0.0)
            elif mode == "near":
                lsn = jnp.where(exists[t], lsn, 0.0)
            hi, lo = _split_hi_lo(lsn)
            lsns.append(lsn)
            sums.append(_dot(hi, uo) + _dot(lo, uo))
        ws = []
        for n in range(len(zs)):
            t = n // len(heads)
            mode = tiles[t][1]
            h = n % len(heads)
            tail = sums[n][:, :tq]
            total = sums[n][:, tq:]
            if carry[h] is not None:
                tail = tail + carry[h]
                total = total + carry[h]
            w = jnp.exp(zs[n] + lsns[n] + tail)
            if mode == "diag":
                w = jnp.where(causal, w, 0.0)
            elif mode == "near":
                w = jnp.where(exists[t], w, 0.0)
            ws.append(w.astype(MXU_DTYPE))
            carry[h] = total
        for t, k0 in enumerate(starts):
            for p, sl in enumerate(pairs):
                vt = v_ref[pl.ds(k0, tq), sl]
                zv = jnp.zeros_like(vt)
                v2 = jnp.concatenate([jnp.where(low, vt, zv), jnp.where(low, zv, vt)], axis=0)
                n = t * len(heads) + 2 * p
                upd = _dot(jnp.concatenate(ws[n:n + 2], axis=1), v2)
                out[p] = upd if out[p] is None else out[p] + upd
        worst = carry[0]
        for h in heads:
            carry_ref[h] = carry[h]
            worst = jnp.maximum(worst, carry[h])
        for p, sl in enumerate(pairs):
            o_ref[:, sl] = out[p]
        return (jnp.max(worst) > EXP_ZERO_BELOW).astype(jnp.int32)

    def cond(st):
        kb, live = st
        return jnp.logical_and(kb >= 0, live > 0)

    def body(st):
        kb, _ = st
        return kb - 1, sweep([(kb, "back")])

    lax.while_loop(cond, body, (i - 3, sweep([(i, "diag"), (i - 1, "near"), (i - 2, "near")])))


def _sb_attention(sbq, sbk, sbv, tq):
    B, T, W = sbq.shape
    r = jnp.arange(tq)
    u = (r[:, None] > r[None, :]).astype(MXU_DTYPE)
    uo = jnp.concatenate([u, jnp.ones((tq, tq), MXU_DTYPE)], axis=1)
    kern = functools.partial(_sb_kernel, tq=tq)
    return pl.pallas_call(
        kern,
        out_shape=jax.ShapeDtypeStruct((B, T, W), F32),
        grid=(B, T // tq),
        in_specs=[pl.BlockSpec((None, tq, W), lambda b, i: (b, i, 0)),
                  pl.BlockSpec((None, T, W), lambda b, i: (b, 0, 0)),
                  pl.BlockSpec((None, T, W), lambda b, i: (b, 0, 0)),
                  pl.BlockSpec((tq, 2 * tq), lambda b, i: (0, 0))],
        out_specs=pl.BlockSpec((None, tq, W), lambda b, i: (b, i, 0)),
        scratch_shapes=[pltpu.VMEM((W // HEAD_DIM, tq, tq), F32)],
        compiler_params=_cparams(("parallel", "parallel")),
        name="sb_attention",
    )(sbq, sbk, sbv, uo)


def _gelu_tanh(x):
    c = math.sqrt(2.0 / math.pi)
    return 0.5 * x * (1.0 + jnp.tanh(c * (x + 0.044715 * (x * x * x))))


def _compress_kernel(kc_ref, vc_ref, wck_ref, w1k_ref, w2k_ref, pk_ref, wcv_ref, w1v_ref, w2v_ref, pv_ref,
                     cos_ref, sin_ref, ko_ref, vo_ref, *, nch):
    rowi = _iota((nch, LANES), 0)

    def one(c_ref, wc_ref, w1_ref, w2_ref, p_ref, rope):
        bias = _dot(p_ref[...], w1_ref[...])[0:1, :]
        proj = jnp.zeros((nch, 2 * NSA_KV_HEADS * CMP_HIDDEN), F32)
        for l in range(CMP_STRIDE):
            x_l = c_ref[pl.ds(l, nch, stride=CMP_STRIDE), :].astype(MXU_DTYPE)
            proj = proj + _dot(x_l, wc_ref[l])
        acc = jnp.zeros((nch, LANES), F32)
        for g in range(NSA_KV_HEADS):
            a = proj[:, (2 * g) * CMP_HIDDEN:(2 * g + 1) * CMP_HIDDEN]
            b = proj[:, (2 * g + 1) * CMP_HIDDEN:(2 * g + 2) * CMP_HIDDEN]
            hid = a + pltpu.roll(b, nch - 1, 0) + bias
            acc = acc + _dot(_gelu_tanh(hid).astype(MXU_DTYPE), w2_ref[g])
        if rope:
            acc = _apply_rope(acc, cos_ref[...], sin_ref[...])
        return jnp.where(rowi < nch - 1, acc, 0.0)

    ko_ref[...] = one(kc_ref, wck_ref, w1k_ref, w2k_ref, pk_ref, True).astype(ko_ref.dtype)
    vo_ref[...] = one(vc_ref, wcv_ref, w1v_ref, w2v_ref, pv_ref, False).T.astype(vo_ref.dtype)


def _pad_w2(w2):
    z = jnp.zeros_like(w2)
    return jnp.stack([jnp.concatenate([w2, z], 1), jnp.concatenate([z, w2], 1)]).astype(MXU_DTYPE)


def _chunk_weights(w1):
    half = CMP_STRIDE * HEAD_DIM
    top = w1[:half].reshape(CMP_STRIDE, HEAD_DIM, CMP_HIDDEN)
    bot = w1[half:].reshape(CMP_STRIDE, HEAD_DIM, CMP_HIDDEN)
    tb = jnp.concatenate([top, bot], axis=2)
    z = jnp.zeros_like(tb)
    return jnp.concatenate([jnp.concatenate([tb, z], axis=2),
                            jnp.concatenate([z, tb], axis=2)], axis=1).astype(MXU_DTYPE)


def _compress(kc, vc, w1k, w2k, pk, w1v, w2v, pv, cos_c, sin_c):
    B, T, _ = kc.shape
    nch = T // CMP_STRIDE
    flat = CMP_STRIDE * HEAD_DIM
    posf = lambda p: jnp.broadcast_to(p.reshape(1, CMP_LEN * HEAD_DIM), (SUBLANES, CMP_LEN * HEAD_DIM)).astype(MXU_DTYPE)
    full = lambda *s: pl.BlockSpec(s, lambda b: (0,) * len(s))
    wide = 2 * NSA_KV_HEADS * CMP_HIDDEN
    kern = functools.partial(_compress_kernel, nch=nch)
    return pl.pallas_call(
        kern,
        out_shape=(jax.ShapeDtypeStruct((B, nch, LANES), MXU_DTYPE),
                   jax.ShapeDtypeStruct((B, LANES, nch), MXU_DTYPE)),
        grid=(B,),
        in_specs=[pl.BlockSpec((None, T, LANES), lambda b: (b, 0, 0)),
                  pl.BlockSpec((None, T, LANES), lambda b: (b, 0, 0)),
                  full(CMP_STRIDE, LANES, wide), full(2 * flat, CMP_HIDDEN), full(2, CMP_HIDDEN, LANES),
                  full(SUBLANES, 2 * flat),
                  full(CMP_STRIDE, LANES, wide), full(2 * flat, CMP_HIDDEN), full(2, CMP_HIDDEN, LANES),
                  full(SUBLANES, 2 * flat),
                  full(nch, LANES), full(nch, LANES)],
        out_specs=(pl.BlockSpec((None, nch, LANES), lambda b: (b, 0, 0)),
                   pl.BlockSpec((None, LANES, nch), lambda b: (b, 0, 0))),
        compiler_params=_cparams(("parallel",)),
        name="nsa_compress",
    )(kc, vc, _chunk_weights(w1k), w1k.astype(MXU_DTYPE), _pad_w2(w2k), posf(pk),
      _chunk_weights(w1v), w1v.astype(MXU_DTYPE), _pad_w2(w2v), posf(pv), cos_c, sin_c)


def _nsa_kernel(q_ref, gt_ref, kcmp_ref, vcmpt_ref, ks_ref, vst_ref, kw_ref, vwt_ref,
                ovt_ref, eselt_ref, o_ref, qa_ref, s_ref, co_ref, *, tq, n_sel, nsp):
    i = pl.program_id(1)
    q0 = i * tq
    ncmp = kcmp_ref.shape[0]
    top = min(SEL_TOPK, n_sel)
    wide = 2 * tq
    heads = range(NSA_HEADS)
    lane = _iota((tq, LANES), 1)

    for g in range(NSA_KV_HEADS):
        in_group = (lane // HEAD_DIM) == g
        for p in range(NSA_GROUP):
            blk = q_ref[:, p * LANES:(p + 1) * LANES]
            qa_ref[g, p * tq:(p + 1) * tq, :LANES] = jnp.where(in_group, blk, jnp.zeros_like(blk))

    def fresh_state():
        return [(jnp.full((1, tq), NEG, F32), jnp.zeros((HEAD_DIM + SUBLANES, tq), F32)) for _ in heads]

    def flash(state, s_ts, bias, vt_ref, k0, w):
        new = []
        for g in range(NSA_KV_HEADS):
            vt = jnp.concatenate([vt_ref[g * HEAD_DIM:(g + 1) * HEAD_DIM, pl.ds(k0, w)],
                                  jnp.ones((SUBLANES, w), MXU_DTYPE)], axis=0)
            for hl in range(NSA_GROUP):
                m_old, acc_old = state[g * NSA_GROUP + hl]
                x = s_ts[g][:, hl * tq:(hl + 1) * tq]
                if bias is not None:
                    x = x + bias
                m_new = jnp.maximum(m_old, jnp.max(x, axis=0, keepdims=True))
                p = jnp.exp(x - m_new).astype(MXU_DTYPE)
                new.append((m_new, jnp.exp(m_old - m_new) * acc_old + _dot(vt, p)))
        return new

    def gated(state, branch):
        res = []
        for h in heads:
            acc = state[h][1]
            r = N_BRANCH * h + branch
            scale = gt_ref[r:r + 1, :] * (1.0 / jnp.maximum(acc[HEAD_DIM:HEAD_DIM + 1, :], 1e-6))
            res.append(acc[:HEAD_DIM, :] * scale)
        return res

    def compress_select(n_blk, n_row):
        kcmp = kcmp_ref[:n_row, :]
        valid_c = (_iota((n_row, tq), 0) * CMP_STRIDE + (CMP_LEN - 1)) <= (q0 + _iota((n_row, tq), 1))
        bias_c = jnp.where(valid_c, 0.0, NEG)
        keep_c = jnp.where(valid_c, 1.0, 0.0)
        n_chunks = n_blk // SUBLANES
        jrow = _iota((SUBLANES, tq), 0)
        qblk = (q0 + _iota((SUBLANES, tq), 1)) // SEL_LEN
        for g in range(NSA_KV_HEADS):
            s_t = _dot_nt(kcmp, qa_ref[g, :, :LANES])
            vct = vcmpt_ref[g * HEAD_DIM:(g + 1) * HEAD_DIM, :n_row]
            psum = None
            for hl in range(NSA_GROUP):
                h = g * NSA_GROUP + hl
                x = s_t[:, hl * tq:(hl + 1) * tq] + bias_c
                p = jnp.exp(x - jnp.max(x, axis=0, keepdims=True)) * keep_c
                p = p * (1.0 / jnp.maximum(jnp.sum(p, axis=0, keepdims=True), 1e-6))
                psum = p if psum is None else psum + p
                co_ref[h] = gt_ref[N_BRANCH * h:N_BRANCH * h + 1, :] * _dot(vct, p.astype(MXU_DTYPE))
            ph, plo = _split_hi_lo(psum)
            ovt = ovt_ref[:n_blk, :n_row]
            imp_t = _dot(ovt, ph) + _dot(ovt, plo)
            scores = []
            for c in range(n_chunks):
                dist = qblk - (jrow + c * SUBLANES)
                forced = (dist >= 0) & (dist < N_LOCAL)
                if c == 0:
                    forced = forced | (jrow == 0)
                imp_c = imp_t[c * SUBLANES:(c + 1) * SUBLANES, :]
                scores.append(jnp.where(dist < 0, -1.0, jnp.where(forced, FORCED_SCORE, imp_c)))
            ranks = [jnp.zeros((SUBLANES, tq), F32) for _ in range(n_chunks)]
            for i2 in range(min(n_blk, n_sel)):
                c2, r2 = divmod(i2, SUBLANES)
                r = jnp.broadcast_to(scores[c2][r2:r2 + 1, :], (SUBLANES, tq))
                for c in range(n_chunks):
                    if c < c2:
                        beats = r > scores[c]
                    elif c > c2:
                        beats = r >= scores[c]
                    else:
                        beats = (r > scores[c]) | ((r == scores[c]) & (jrow > r2))
                    ranks[c] = ranks[c] + jnp.where(beats, 1.0, 0.0)
            nsel = [jnp.where((ranks[c] < top) & (scores[c] >= 0.0), 0.0, NEG) for c in range(n_chunks)]
            if n_blk < nsp:
                nsel.append(jnp.full((nsp - n_blk, tq), NEG, F32))
            if nsp < LANES:
                nsel.append(jnp.zeros((LANES - nsp, tq), F32))
            nsel_q = jnp.concatenate(nsel, axis=0).T.astype(MXU_DTYPE)
            for p in range(NSA_GROUP):
                qa_ref[g, p * tq:(p + 1) * tq, LANES:] = nsel_q

    cls_blk = 2 * SUBLANES
    n_cls = -(-nsp // cls_blk)
    cls = jnp.minimum((2 * i + 1) // cls_blk, n_cls - 1)
    for k in range(n_cls):
        n_blk = min(cls_blk * (k + 1), nsp)
        n_row = min(-(-n_blk * (SEL_LEN // CMP_STRIDE) // LANES) * LANES, ncmp)

        @pl.when(cls == k)
        def _():
            compress_select(n_blk, n_row)

    n_win = WINDOW // tq

    def win_scores(j):
        k0 = pl.multiple_of(jnp.maximum(q0 - (n_win - j) * tq, 0), tq)
        kt = kw_ref[pl.ds(k0, tq), :]
        return [_dot_nt(kt, qa_ref[g, :, :LANES]) for g in range(NSA_KV_HEADS)], k0

    krow = _iota((tq, tq), 0)
    qcol = _iota((tq, tq), 1)
    causal_bias = jnp.where(krow <= qcol, 0.0, NEG)
    oldest_bias = jnp.where(krow > qcol, 0.0, NEG)
    w_scores, w_k0 = win_scores(0)
    win_state = fresh_state()
    for j in range(n_win + 1):
        nxt = win_scores(j + 1) if j < n_win else None
        bias = oldest_bias if j == 0 else (causal_bias if j == n_win else None)
        new_state = flash(win_state, w_scores, bias, vwt_ref, w_k0, tq)
        if j < n_win:
            live = i >= n_win - j
            new_state = [(jnp.where(live, mn, mo), jnp.where(live, an, ao))
                         for (mn, an), (mo, ao) in zip(new_state, win_state)]
            w_scores, w_k0 = nxt
        win_state = new_state
    outs = gated(win_state, 2)

    t_max = eselt_ref.shape[0] // wide - 1

    def tile_start(t):
        return pl.multiple_of(jnp.minimum(t, t_max) * wide, wide)

    def sel_scores(t, slot):
        k0 = tile_start(t)
        kaug = jnp.concatenate([ks_ref[pl.ds(k0, wide), :], eselt_ref[pl.ds(k0, wide), :]], axis=1)
        for g in range(NSA_KV_HEADS):
            s_ref[slot, g] = _dot_nt(kaug, qa_ref[g])

    def sel_flash(state, t, slot, causal):
        bias = None
        if causal:
            bias = jnp.where(_iota((wide, tq), 0) - _iota((wide, tq), 1) <= q0 - t * wide, 0.0, NEG)
        s_ts = [s_ref[slot, g] for g in range(NSA_KV_HEADS)]
        return flash(state, s_ts, bias, vst_ref, tile_start(t), wide)

    def sel_body(u, state):
        sel_scores(2 * u + 1, 1)
        state = sel_flash(state, 2 * u, 0, False)
        sel_scores(2 * u + 2, 0)
        return sel_flash(state, 2 * u + 1, 1, False)

    n_tiles = i // 2 + 1
    n_even = n_tiles + n_tiles % 2
    sel_scores(0, 0)
    sel_state = lax.fori_loop(0, n_even // 2 - 1, sel_body, fresh_state())
    sel_scores(n_even - 1, 1)
    sel_state = sel_flash(sel_state, n_even - 2, 0, True)

    sel_state = sel_flash(sel_state, n_even - 1, 1, True)
    for h, res in enumerate(gated(sel_state, 1)):
        outs[h] = outs[h] + res

    for hl in range(NSA_GROUP):
        pair_t = jnp.concatenate([outs[hl] + co_ref[hl],
                                  outs[NSA_GROUP + hl] + co_ref[NSA_GROUP + hl]], axis=0)
        o_ref[:, hl * LANES:(hl + 1) * LANES] = pair_t.T


def _nsa_attention(nq, gates_t, kcmp, vcmp_t, ks, vs_t, kw, vw_t, tq):
    B, T, _ = nq.shape
    ncmp = kcmp.shape[1]
    n_sel = T // SEL_LEN
    nsp = -(-n_sel // SUBLANES) * SUBLANES
    cs = jnp.arange(ncmp) * CMP_STRIDE
    ss = jnp.arange(nsp) * SEL_LEN
    ov = jnp.clip(jnp.minimum(cs[None, :] + CMP_LEN, ss[:, None] + SEL_LEN)
                  - jnp.maximum(cs[None, :], ss[:, None]), 0, None).astype(F32) / CMP_LEN
    ov = jnp.where((jnp.arange(nsp) < n_sel)[:, None] & (jnp.arange(ncmp) < ncmp - 1)[None, :], ov, 0.0)
    ovt = ov.astype(MXU_DTYPE)
    esel_t = ((jnp.arange(T) // SEL_LEN)[:, None] == jnp.arange(LANES)[None, :]).astype(MXU_DTYPE)
    kern = functools.partial(_nsa_kernel, tq=tq, n_sel=n_sel, nsp=nsp)
    per_b = lambda r, w: pl.BlockSpec((None, r, w), lambda b, i: (b, 0, 0))
    return pl.pallas_call(
        kern,
        out_shape=jax.ShapeDtypeStruct((B, T, NSA_WIDTH), F32),
        grid=(B, T // tq),
        in_specs=[pl.BlockSpec((None, tq, NSA_WIDTH), lambda b, i: (b, i, 0)),
                  pl.BlockSpec((None, LANES, tq), lambda b, i: (b, 0, i)),
                  per_b(ncmp, LANES), per_b(LANES, ncmp),
                  per_b(T, LANES), per_b(LANES, T), per_b(T, LANES), per_b(LANES, T),
                  pl.BlockSpec((nsp, ncmp), lambda b, i: (0, 0)),
                  pl.BlockSpec((T, LANES), lambda b, i: (0, 0))],
        out_specs=pl.BlockSpec((None, tq, NSA_WIDTH), lambda b, i: (b, i, 0)),
        scratch_shapes=[pltpu.VMEM((NSA_KV_HEADS, NSA_GROUP * tq, 2 * LANES), MXU_DTYPE),
                        pltpu.VMEM((2, NSA_KV_HEADS, 2 * tq, NSA_GROUP * tq), F32),
                        pltpu.VMEM((NSA_HEADS, HEAD_DIM, tq), F32)],
        compiler_params=_cparams(("parallel", "parallel")),
        name="nsa_attention",
    )(nq, gates_t, kcmp, vcmp_t, ks, vs_t, kw, vw_t, ovt, esel_t)


def _rms(v, g):
    return (v * lax.rsqrt(jnp.mean(v * v, axis=-1, keepdims=True) + EPS)) * g


def _ffn_kernel(x_ref, osb_ref, onsa_ref, mod_ref, gsb_ref, gnsa_ref, wo_ref, g_ref, win_ref, cw_ref,
                wdn_ref, fg_ref, o_ref, prev_ref, ubuf_ref, *, tm, fc, final_norm):
    t = pl.program_id(1)

    @pl.when(t == 0)
    def _():
        prev_ref[...] = jnp.zeros_like(prev_ref)

    a_sb = _rms(osb_ref[...], gsb_ref[...]).astype(MXU_DTYPE)
    a_nsa = _rms(onsa_ref[...], gnsa_ref[...]).astype(MXU_DTYPE)
    attn = _dot(a_sb, wo_ref[:SB_WIDTH, :]) + _dot(a_nsa, wo_ref[SB_WIDTH:, :])
    x = x_ref[...] + mod_ref[2:3, :] * attn
    h = _rms(x, g_ref[...]) * (1.0 + mod_ref[4:5, :]) + mod_ref[3:4, :]
    hb = h.astype(MXU_DTYPE)
    def conv(u, c0, slot):
        ubuf_ref[slot, :SUBLANES, :] = prev_ref[:, c0:c0 + fc]
        ubuf_ref[slot, SUBLANES:, :] = u
        prev_ref[:, c0:c0 + fc] = u[tm - SUBLANES:, :]
        u1 = ubuf_ref[slot, SUBLANES - 1:SUBLANES - 1 + tm, :]
        u2 = ubuf_ref[slot, SUBLANES - 2:SUBLANES - 2 + tm, :]
        cw = cw_ref[:, c0:c0 + fc]
        return cw[2:3, :] * u + cw[1:2, :] * u1 + cw[0:1, :] * u2 + cw[3:4, :]

    def up(c):
        return (_dot(hb, win_ref[:, c * fc:(c + 1) * fc]),
                _dot(hb, win_ref[:, D_FF + c * fc:D_FF + (c + 1) * fc]))

    n_chunks = D_FF // fc
    acc = jnp.zeros((tm, D_MODEL), F32)
    ua, ub = up(0)
    for c in range(n_chunks):
        nxt = up(c + 1) if c + 1 < n_chunks else None
        ya = conv(ua, c * fc, 2 * (c % 2))
        yb = conv(ub, D_FF + c * fc, 2 * (c % 2) + 1)
        z = (ya * jax.nn.sigmoid(ya)) * yb
        acc = acc + _dot(z.astype(MXU_DTYPE), wdn_ref[c * fc:(c + 1) * fc, :])
        if nxt is not None:
            ua, ub = nxt
    y = x + mod_ref[5:6, :] * acc
    if final_norm:
        y = _rms(y, fg_ref[...])
    o_ref[...] = y


def _outproj_ffn(x, o_sb, o_nsa, mod_l, g_sb, g_nsa, w_out, ln_g, w_in, conv_wb, w_down, final_g,
                 tm, fc, final_norm):
    B, T, D = x.shape
    row = lambda w: pl.BlockSpec((None, tm, w), lambda b, t: (b, t, 0))
    vec = lambda w: pl.BlockSpec((1, w), lambda b, t: (0, 0))
    const = lambda r, c: pl.BlockSpec((r, c), lambda b, t: (0, 0), pipeline_mode=pl.Buffered(1))
    kern = functools.partial(_ffn_kernel, tm=tm, fc=fc, final_norm=final_norm)
    return pl.pallas_call(
        kern,
        out_shape=jax.ShapeDtypeStruct((B, T, D), F32),
        grid=(B, T // tm),
        in_specs=[row(D), row(SB_WIDTH), row(NSA_WIDTH),
                  pl.BlockSpec((None, 6, D), lambda b, t: (b, 0, 0)),
                  vec(SB_WIDTH), vec(NSA_WIDTH), const(D, D), vec(D),
                  const(D, 2 * D_FF), const(SUBLANES, 2 * D_FF), const(D_FF, D), vec(D)],
        out_specs=row(D),
        scratch_shapes=[pltpu.VMEM((SUBLANES, 2 * D_FF), F32),
                        pltpu.VMEM((4, SUBLANES + tm, fc), F32)],
        compiler_params=_cparams(("parallel", "arbitrary")),
        name="outproj_ffn",
    )(x, o_sb, o_nsa, mod_l, g_sb, g_nsa, w_out, ln_g, w_in, conv_wb, w_down, final_g)


_NSA_HEAD_ORDER = [h for p in range(NSA_GROUP) for h in (p, NSA_GROUP + p)]


def _head_cols(order):
    return jnp.asarray([h * HEAD_DIM + d for h in order for d in range(HEAD_DIM)], jnp.int32)


def _prep_w_in(w):
    nq0 = 3 * SB_WIDTH
    kv0 = nq0 + NSA_WIDTH
    gl0 = kv0 + 2 * N_BRANCH * KV_WIDTH
    nq = w[:, nq0:kv0][:, _head_cols(_NSA_HEAD_ORDER)]
    gl = jnp.pad(w[:, gl0:], ((0, 0), (0, LANES - (w.shape[1] - gl0))))
    return jnp.concatenate([w[:, :nq0], nq, w[:, kv0:gl0], gl], axis=1).astype(MXU_DTYPE)


def kernel(x, c, ln1_g, ln2_g, w_ada, b_ada, w_in, cmp_pos_k, cmp_w1_k, cmp_w2_k, cmp_pos_v, cmp_w1_v, cmp_w2_v, sb_out_g, nsa_out_g, w_out, ffn_w_in, ffn_conv_w, ffn_conv_b, ffn_w_down, final_g):
    B, T, D = x.shape
    L = w_in.shape[0]
    assert D == D_MODEL and T % 512 == 0 and T // SEL_LEN <= LANES
    tm_proj = 512
    tm_ffn = 256
    fc = 256
    tq = 128
    nch = T // CMP_STRIDE

    half = HEAD_DIM // 2
    inv = ROPE_THETA ** (-jnp.arange(half, dtype=F32) / half)
    inv_lanes = jnp.tile(inv, LANES // half).reshape(1, LANES)
    cos_t, sin_t = _rope_tables(inv_lanes, T, 1, 0)
    cos_c, sin_c = _rope_tables(inv_lanes, nch, CMP_STRIDE, CMP_LEN - 1)

    mod = _modulation(c, w_ada, b_ada).reshape(L, B, 6, D)
    nsa_cols = _head_cols(_NSA_HEAD_ORDER)

    for l in range(L):
        mod_l = mod[l]
        (sbq, sbk, sbv, nq, kc, vc, ks, vs, kw, vw, gates) = _projection(
            x, mod_l, ln1_g[l].reshape(1, D), _prep_w_in(w_in[l]), cos_t, sin_t, tm_proj)
        o_sb = _sb_attention(sbq, sbk, sbv, tq)
        kcmp, vcmp = _compress(kc, vc, cmp_w1_k[l], cmp_w2_k[l], cmp_pos_k[l],
                               cmp_w1_v[l], cmp_w2_v[l], cmp_pos_v[l], cos_c, sin_c)
        o_nsa = _nsa_attention(nq, gates, kcmp, vcmp, ks, vs, kw, vw, tq)
        w_o = jnp.concatenate([w_out[l][:SB_WIDTH], w_out[l][SB_WIDTH:][nsa_cols]], axis=0).astype(MXU_DTYPE)
        conv_wb = jnp.concatenate([ffn_conv_w[l], ffn_conv_b[l][None, :],
                                   jnp.zeros((SUBLANES - CONV_W - 1, 2 * D_FF), F32)], axis=0)
        x = _outproj_ffn(x, o_sb, o_nsa, mod_l, sb_out_g[l].reshape(1, SB_WIDTH),
                         nsa_out_g[l][nsa_cols].reshape(1, NSA_WIDTH), w_o,
                         ln2_g[l].reshape(1, D), ffn_w_in[l].astype(MXU_DTYPE), conv_wb,
                         ffn_w_down[l].astype(MXU_DTYPE), final_g.reshape(1, D), tm_ffn, fc, l == L - 1)
    return x
```

```python
import functools
import math

import jax
import jax.numpy as jnp
from jax import lax
from jax.experimental import pallas as pl
from jax.experimental.pallas import tpu as pltpu

D_MODEL = 1024
HEAD_DIM = 64
SB_HEADS = 8
NSA_HEADS = 8
NSA_KV_HEADS = 2
NSA_GROUP = NSA_HEADS // NSA_KV_HEADS
SB_WIDTH = SB_HEADS * HEAD_DIM
NSA_WIDTH = NSA_HEADS * HEAD_DIM
KV_WIDTH = NSA_KV_HEADS * HEAD_DIM
N_BRANCH = 3
CMP_LEN = 32
CMP_STRIDE = 16
CMP_HIDDEN = 256
SEL_LEN = 64
SEL_TOPK = 16
N_LOCAL = 2
WINDOW = 512
ROPE_THETA = 10000.0
D_FF = 2816
CONV_W = 3
EPS = 1e-6
NEG = -1e30
FORCED_SCORE = 1e6

F32 = jnp.float32
MXU_DTYPE = jnp.bfloat16
LANES = 128
SUBLANES = 8
VMEM_LIMIT = 56 * 1024 * 1024
QK_SCALE = HEAD_DIM ** -0.5
EXP_ZERO_BELOW = -104.0

_NT = (((1,), (1,)), ((), ()))


def _cparams(sem):
    return pltpu.CompilerParams(dimension_semantics=sem, vmem_limit_bytes=VMEM_LIMIT)


def _dot(a, b):
    return jnp.dot(a, b, preferred_element_type=F32)


def _dot_nt(a, b):
    return lax.dot_general(a, b, _NT, preferred_element_type=F32)


def _split_hi_lo(a):
    hi = a.astype(MXU_DTYPE)
    lo = (a - hi.astype(F32)).astype(MXU_DTYPE)
    return hi, lo


def _iota(shape, dim):
    return lax.broadcasted_iota(jnp.int32, shape, dim)


def _mod_kernel(c_ref, w_ref, b_ref, o_ref):
    c = c_ref[...]
    a = c * jax.nn.sigmoid(c)
    o_ref[...] = jnp.dot(a, w_ref[...], preferred_element_type=F32,
                         precision=lax.Precision.HIGHEST) + b_ref[...]


def _modulation(c, w_ada, b_ada):
    L, D, D6 = w_ada.shape
    B = c.shape[0]
    n = D6 // D
    return pl.pallas_call(
        _mod_kernel,
        out_shape=jax.ShapeDtypeStruct((L, B, D6), F32),
        grid=(L, n),
        in_specs=[pl.BlockSpec((B, D), lambda l, j: (0, 0)),
                  pl.BlockSpec((None, D, D), lambda l, j: (l, 0, j)),
                  pl.BlockSpec((None, 1, D), lambda l, j: (l, 0, j))],
        out_specs=pl.BlockSpec((None, B, D), lambda l, j: (l, 0, j)),
        compiler_params=_cparams(("parallel", "parallel")),
        name="adaln_mod",
    )(c, w_ada, b_ada.reshape(L, 1, D6))


def _rope_table_kernel(inv_ref, cos_ref, sin_ref, *, rows, stride, offset):
    i = pl.program_id(0)
    r = _iota((rows, LANES), 0) + i * rows
    pos = (r * stride + offset).astype(F32)
    ang = pos * inv_ref[...]
    lane = _iota((rows, LANES), 1)
    first = (lane % HEAD_DIM) < (HEAD_DIM // 2)
    s = jnp.sin(ang)
    cos_ref[...] = jnp.cos(ang)
    sin_ref[...] = jnp.where(first, -s, s)


def _rope_tables(inv_lanes, n_rows, stride, offset):
    rows = min(n_rows, 512)
    kern = functools.partial(_rope_table_kernel, rows=rows, stride=stride, offset=offset)
    return pl.pallas_call(
        kern,
        out_shape=(jax.ShapeDtypeStruct((n_rows, LANES), F32),) * 2,
        grid=(n_rows // rows,),
        in_specs=[pl.BlockSpec((1, LANES), lambda i: (0, 0))],
        out_specs=(pl.BlockSpec((rows, LANES), lambda i: (i, 0)),) * 2,
        compiler_params=_cparams(("parallel",)),
        name="rope_tables",
    )(inv_lanes)


def _apply_rope(v, cos, sin_signed):
    lane = _iota(v.shape, 1)
    first = (lane % HEAD_DIM) < (HEAD_DIM // 2)
    half = HEAD_DIM // 2
    partner = jnp.where(first, pltpu.roll(v, LANES - half, 1), pltpu.roll(v, half, 1))
    return v * cos + partner * sin_signed


_PROJ_COLS = 3 * SB_WIDTH + NSA_WIDTH + 2 * N_BRANCH * KV_WIDTH + LANES


def _proj_kernel(x_ref, mod_ref, g_ref, w_ref, cos_ref, sin_ref,
                 sbq_ref, sbk_ref, sbv_ref, nq_ref, kc_ref, vc_ref,
                 ks_ref, vs_ref, kw_ref, vw_ref, gate_ref):
    x = x_ref[...]
    shift = mod_ref[0:1, :]
    scale = mod_ref[1:2, :]
    ms = jnp.mean(x * x, axis=-1, keepdims=True)
    h = (x * lax.rsqrt(ms + EPS)) * g_ref[...]
    h = h * (1.0 + scale) + shift
    hb = h.astype(MXU_DTYPE)
    cos = cos_ref[...]
    sin = sin_ref[...]

    def mm(lo, width):
        return _dot(hb, w_ref[:, lo:lo + width])

    o = 0
    sbq_ref[...] = (mm(o, SB_WIDTH) * QK_SCALE).astype(sbq_ref.dtype); o += SB_WIDTH
    sbk_ref[...] = mm(o, SB_WIDTH).astype(sbk_ref.dtype); o += SB_WIDTH
    sbv_ref[...] = mm(o, SB_WIDTH).astype(sbv_ref.dtype); o += SB_WIDTH
    nq = mm(o, NSA_WIDTH); o += NSA_WIDTH
    for p in range(NSA_WIDTH // LANES):
        blk = _apply_rope(nq[:, p * LANES:(p + 1) * LANES], cos, sin) * QK_SCALE
        nq_ref[:, p * LANES:(p + 1) * LANES] = blk.astype(nq_ref.dtype)
    rest = mm(o, 2 * N_BRANCH * KV_WIDTH + LANES)
    kc_ref[...] = rest[:, 0 * LANES:1 * LANES].astype(kc_ref.dtype)
    vc_ref[...] = rest[:, 1 * LANES:2 * LANES].astype(vc_ref.dtype)
    ks_ref[...] = _apply_rope(rest[:, 2 * LANES:3 * LANES], cos, sin).astype(ks_ref.dtype)
    kw_ref[...] = _apply_rope(rest[:, 4 * LANES:5 * LANES], cos, sin).astype(kw_ref.dtype)
    vs_ref[...] = rest[:, 3 * LANES:4 * LANES].T.astype(vs_ref.dtype)
    vw_ref[...] = rest[:, 5 * LANES:6 * LANES].T.astype(vw_ref.dtype)
    gate_ref[...] = jax.nn.sigmoid(rest[:, 6 * LANES:7 * LANES]).T


def _projection(x, mod_l, ln_g, w_cat, cos_t, sin_t, tm):
    B, T, D = x.shape
    row = lambda w: pl.BlockSpec((None, tm, w), lambda b, t: (b, t, 0))
    col = pl.BlockSpec((None, LANES, tm), lambda b, t: (b, 0, t))
    shp = lambda w, dt: jax.ShapeDtypeStruct((B, T, w), dt)
    shp_t = lambda dt: jax.ShapeDtypeStruct((B, LANES, T), dt)
    md = MXU_DTYPE
    return pl.pallas_call(
        _proj_kernel,
        out_shape=(shp(SB_WIDTH, md), shp(SB_WIDTH, md), shp(SB_WIDTH, md), shp(NSA_WIDTH, md),
                   shp(LANES, F32), shp(LANES, F32), shp(LANES, md), shp_t(md),
                   shp(LANES, md), shp_t(md), shp_t(F32)),
        grid=(B, T // tm),
        in_specs=[row(D),
                  pl.BlockSpec((None, 6, D), lambda b, t: (b, 0, 0)),
                  pl.BlockSpec((1, D), lambda b, t: (0, 0)),
                  pl.BlockSpec((D, _PROJ_COLS), lambda b, t: (0, 0)),
                  pl.BlockSpec((tm, LANES), lambda b, t: (t, 0)),
                  pl.BlockSpec((tm, LANES), lambda b, t: (t, 0))],
        out_specs=(row(SB_WIDTH), row(SB_WIDTH), row(SB_WIDTH), row(NSA_WIDTH),
                   row(LANES), row(LANES), row(LANES), col, row(LANES), col, col),
        compiler_params=_cparams(("parallel", "parallel")),
        name="norm_proj",
    )(x, mod_l, ln_g, w_cat, cos_t, sin_t)


def _sb_kernel(q_ref, k_ref, v_ref, uo_ref, o_ref, carry_ref, *, tq):
    n_pairs = q_ref.shape[1] // LANES
    low = _iota((tq, LANES), 1) < HEAD_DIM
    causal = _iota((tq, tq), 1) < _iota((tq, tq), 0)

    def sweep(tiles, rows):
        pairs = [slice(p * LANES, (p + 1) * LANES) for p in range(n_pairs)]
        heads = range(2 * n_pairs)
        resume = tiles[0][1] == "back"
        carry = [carry_ref[h] if resume else None for h in heads]
        out = [o_ref[rows, sl] if resume else None for sl in pairs]
        exists = [kb >= 0 for kb, _ in tiles]
        uo = uo_ref[...]
        starts = [pl.multiple_of(jnp.maximum(kb, 0) * tq, tq) for kb, _ in tiles]
        zs = []
        for k0 in starts:
            for sl in pairs:
                q = q_ref[rows, sl]
                kt = k_ref[pl.ds(k0, tq), sl]
                zq = jnp.zeros_like(q)
                zs.append(_dot_nt(jnp.where(low, q, zq), kt))
                zs.append(_dot_nt(jnp.where(low, zq, q), kt))
        lsns, sums = [], []
        for n, z in enumerate(zs):
            t = n // len(heads)
            mode = tiles[t][1]
            lsn = -(jnp.maximum(z, 0.0) + jnp.log(1.0 + jnp.exp(-jnp.abs(z))))
            if mode == "diag":
                lsn = jnp.where(causal, lsn, 0.0)
            elif mode == "near":
                lsn = jnp.where(exists[t], lsn, 0.0)
            hi, lo = _split_hi_lo(lsn)
            lsns.append(lsn)
            sums.append(_dot(hi, uo) + _dot(lo, uo))
        ws = []
        for n in range(len(zs)):
            t = n // len(heads)
            mode = tiles[t][1]
            h = n % len(heads)
            tail = sums[n][:, :tq]
            total = sums[n][:, tq:]
            if carry[h] is not None:
                tail = tail + carry[h]
                total = total + carry[h]
            w = jnp.exp(zs[n] + lsns[n] + tail)
            if mode == "diag":
                w = jnp.where(causal, w, 0.0)
            elif mode == "near":
                w = jnp.where(exists[t], w, 0.0)
            ws.append(w.astype(MXU_DTYPE))
            carry[h] = total
        for t, k0 in enumerate(starts):
            for p, sl in enumerate(pairs):
                vt = v_ref[pl.ds(k0, tq), sl]
                zv = jnp.zeros_like(vt)
                v2 = jnp.concatenate([jnp.where(low, vt, zv), jnp.where(low, zv, vt)], axis=0)
                n = t * len(heads) + 2 * p
                upd = _dot(jnp.concatenate(ws[n:n + 2], axis=1), v2)
                out[p] = upd if out[p] is None else out[p] + upd
        worst = carry[0]
        for h in heads:
            carry_ref[h] = carry[h]
            worst = jnp.maximum(worst, carry[h])
        for p, sl in enumerate(pairs):
            o_ref[rows, sl] = out[p]
        return (jnp.max(worst) > EXP_ZERO_BELOW).astype(jnp.int32)

    def cond(st):
        kb, live = st
        return jnp.logical_and(kb >= 0, live > 0)

    for sub in range(q_ref.shape[0] // tq):
        i = pl.program_id(1) * (q_ref.shape[0] // tq) + sub
        rows = slice(sub * tq, (sub + 1) * tq)

        def body(st):
            kb, _ = st
            return kb - 1, sweep([(kb, "back")], rows)

        first = sweep([(i, "diag"), (i - 1, "near"), (i - 2, "near")], rows)
        lax.while_loop(cond, body, (i - 3, first))


def _sb_attention(sbq, sbk, sbv, tq, rows_per_step):
    B, T, W = sbq.shape
    r = jnp.arange(tq)
    u = (r[:, None] > r[None, :]).astype(MXU_DTYPE)
    uo = jnp.concatenate([u, jnp.ones((tq, tq), MXU_DTYPE)], axis=1)
    kern = functools.partial(_sb_kernel, tq=tq)
    return pl.pallas_call(
        kern,
        out_shape=jax.ShapeDtypeStruct((B, T, W), F32),
        grid=(B, T // rows_per_step),
        in_specs=[pl.BlockSpec((None, rows_per_step, W), lambda b, i: (b, i, 0)),
                  pl.BlockSpec((None, T, W), lambda b, i: (b, 0, 0)),
                  pl.BlockSpec((None, T, W), lambda b, i: (b, 0, 0)),
                  pl.BlockSpec((tq, 2 * tq), lambda b, i: (0, 0))],
        out_specs=pl.BlockSpec((None, rows_per_step, W), lambda b, i: (b, i, 0)),
        scratch_shapes=[pltpu.VMEM((W // HEAD_DIM, tq, tq), F32)],
        compiler_params=_cparams(("parallel", "parallel")),
        name="sb_attention",
    )(sbq, sbk, sbv, uo)


def _gelu_tanh(x):
    c = math.sqrt(2.0 / math.pi)
    return 0.5 * x * (1.0 + jnp.tanh(c * (x + 0.044715 * (x * x * x))))


def _compress_kernel(kc_ref, vc_ref, wck_ref, w1k_ref, w2k_ref, pk_ref, wcv_ref, w1v_ref, w2v_ref, pv_ref,
                     cos_ref, sin_ref, ko_ref, vo_ref, *, nch):
    rowi = _iota((nch, LANES), 0)

    def one(c_ref, wc_ref, w1_ref, w2_ref, p_ref, rope):
        bias = _dot(p_ref[...], w1_ref[...])[0:1, :]
        proj = jnp.zeros((nch, 2 * NSA_KV_HEADS * CMP_HIDDEN), F32)
        for l in range(CMP_STRIDE):
            x_l = c_ref[pl.ds(l, nch, stride=CMP_STRIDE), :].astype(MXU_DTYPE)
            proj = proj + _dot(x_l, wc_ref[l])
        acc = jnp.zeros((nch, LANES), F32)
        for g in range(NSA_KV_HEADS):
            a = proj[:, (2 * g) * CMP_HIDDEN:(2 * g + 1) * CMP_HIDDEN]
            b = proj[:, (2 * g + 1) * CMP_HIDDEN:(2 * g + 2) * CMP_HIDDEN]
            hid = a + pltpu.roll(b, nch - 1, 0) + bias
            acc = acc + _dot(_gelu_tanh(hid).astype(MXU_DTYPE), w2_ref[g])
        if rope:
            acc = _apply_rope(acc, cos_ref[...], sin_ref[...])
        return jnp.where(rowi < nch - 1, acc, 0.0)

    ko_ref[...] = one(kc_ref, wck_ref, w1k_ref, w2k_ref, pk_ref, True).astype(ko_ref.dtype)
    vo_ref[...] = one(vc_ref, wcv_ref, w1v_ref, w2v_ref, pv_ref, False).T.astype(vo_ref.dtype)


def _pad_w2(w2):
    z = jnp.zeros_like(w2)
    return jnp.stack([jnp.concatenate([w2, z], 1), jnp.concatenate([z, w2], 1)]).astype(MXU_DTYPE)


def _chunk_weights(w1):
    half = CMP_STRIDE * HEAD_DIM
    top = w1[:half].reshape(CMP_STRIDE, HEAD_DIM, CMP_HIDDEN)
    bot = w1[half:].reshape(CMP_STRIDE, HEAD_DIM, CMP_HIDDEN)
    tb = jnp.concatenate([top, bot], axis=2)
    z = jnp.zeros_like(tb)
    return jnp.concatenate([jnp.concatenate([tb, z], axis=2),
                            jnp.concatenate([z, tb], axis=2)], axis=1).astype(MXU_DTYPE)


def _compress(kc, vc, w1k, w2k, pk, w1v, w2v, pv, cos_c, sin_c):
    B, T, _ = kc.shape
    nch = T // CMP_STRIDE
    flat = CMP_STRIDE * HEAD_DIM
    posf = lambda p: jnp.broadcast_to(p.reshape(1, CMP_LEN * HEAD_DIM), (SUBLANES, CMP_LEN * HEAD_DIM)).astype(MXU_DTYPE)
    full = lambda *s: pl.BlockSpec(s, lambda b: (0,) * len(s))
    wide = 2 * NSA_KV_HEADS * CMP_HIDDEN
    kern = functools.partial(_compress_kernel, nch=nch)
    return pl.pallas_call(
        kern,
        out_shape=(jax.ShapeDtypeStruct((B, nch, LANES), MXU_DTYPE),
                   jax.ShapeDtypeStruct((B, LANES, nch), MXU_DTYPE)),
        grid=(B,),
        in_specs=[pl.BlockSpec((None, T, LANES), lambda b: (b, 0, 0)),
                  pl.BlockSpec((None, T, LANES), lambda b: (b, 0, 0)),
                  full(CMP_STRIDE, LANES, wide), full(2 * flat, CMP_HIDDEN), full(2, CMP_HIDDEN, LANES),
                  full(SUBLANES, 2 * flat),
                  full(CMP_STRIDE, LANES, wide), full(2 * flat, CMP_HIDDEN), full(2, CMP_HIDDEN, LANES),
                  full(SUBLANES, 2 * flat),
                  full(nch, LANES), full(nch, LANES)],
        out_specs=(pl.BlockSpec((None, nch, LANES), lambda b: (b, 0, 0)),
                   pl.BlockSpec((None, LANES, nch), lambda b: (b, 0, 0))),
        compiler_params=_cparams(("parallel",)),
        name="nsa_compress",
    )(kc, vc, _chunk_weights(w1k), w1k.astype(MXU_DTYPE), _pad_w2(w2k), posf(pk),
      _chunk_weights(w1v), w1v.astype(MXU_DTYPE), _pad_w2(w2v), posf(pv), cos_c, sin_c)


def _nsa_kernel(q_ref, gt_ref, kcmp_ref, vcmpt_ref, ks_ref, vst_ref, kw_ref, vwt_ref,
                ovt_ref, eselt_ref, o_ref, qa_ref, s_ref, co_ref, *, tq, n_sel, nsp):
    i = pl.program_id(1)
    q0 = i * tq
    ncmp = kcmp_ref.shape[0]
    top = min(SEL_TOPK, n_sel)
    wide = 2 * tq
    heads = range(NSA_HEADS)
    lane = _iota((tq, LANES), 1)

    for g in range(NSA_KV_HEADS):
        in_group = (lane // HEAD_DIM) == g
        for p in range(NSA_GROUP):
            blk = q_ref[:, p * LANES:(p + 1) * LANES]
            qa_ref[g, p * tq:(p + 1) * tq, :LANES] = jnp.where(in_group, blk, jnp.zeros_like(blk))

    def fresh_state():
        return [(jnp.full((1, tq), NEG, F32), jnp.zeros((HEAD_DIM + SUBLANES, tq), F32)) for _ in heads]

    def flash(state, s_ts, bias, vt_ref, k0, w):
        new = []
        for g in range(NSA_KV_HEADS):
            vt = jnp.concatenate([vt_ref[g * HEAD_DIM:(g + 1) * HEAD_DIM, pl.ds(k0, w)],
                                  jnp.ones((SUBLANES, w), MXU_DTYPE)], axis=0)
            for hl in range(NSA_GROUP):
                m_old, acc_old = state[g * NSA_GROUP + hl]
                x = s_ts[g][:, hl * tq:(hl + 1) * tq]
                if bias is not None:
                    x = x + bias
                m_new = jnp.maximum(m_old, jnp.max(x, axis=0, keepdims=True))
                p = jnp.exp(x - m_new).astype(MXU_DTYPE)
                new.append((m_new, jnp.exp(m_old - m_new) * acc_old + _dot(vt, p)))
        return new

    def gated(state, branch):
        res = []
        for h in heads:
            acc = state[h][1]
            r = N_BRANCH * h + branch
            scale = gt_ref[r:r + 1, :] * (1.0 / jnp.maximum(acc[HEAD_DIM:HEAD_DIM + 1, :], 1e-6))
            res.append(acc[:HEAD_DIM, :] * scale)
        return res

    n_win = WINDOW // tq

    def win_scores(j):
        k0 = pl.multiple_of(jnp.maximum(q0 - (n_win - j) * tq, 0), tq)
        kt = kw_ref[pl.ds(k0, tq), :]
        return [_dot_nt(kt, qa_ref[g, :, :LANES]) for g in range(NSA_KV_HEADS)], k0

    def window_branch():
        krow = _iota((tq, tq), 0)
        qcol = _iota((tq, tq), 1)
        causal_bias = jnp.where(krow <= qcol, 0.0, NEG)
        oldest_bias = jnp.where(krow > qcol, 0.0, NEG)
        w_scores, w_k0 = win_scores(0)
        win_state = fresh_state()
        for j in range(n_win + 1):
            nxt = win_scores(j + 1) if j < n_win else None
            bias = oldest_bias if j == 0 else (causal_bias if j == n_win else None)
            new_state = flash(win_state, w_scores, bias, vwt_ref, w_k0, tq)
            if j < n_win:
                live = i >= n_win - j
                new_state = [(jnp.where(live, mn, mo), jnp.where(live, an, ao))
                             for (mn, an), (mo, ao) in zip(new_state, win_state)]
                w_scores, w_k0 = nxt
            win_state = new_state
        return gated(win_state, 2)

    def compress_select(n_blk, n_row):
        kcmp = kcmp_ref[:n_row, :]
        valid_c = (_iota((n_row, tq), 0) * CMP_STRIDE + (CMP_LEN - 1)) <= (q0 + _iota((n_row, tq), 1))
        bias_c = jnp.where(valid_c, 0.0, NEG)
        keep_c = jnp.where(valid_c, 1.0, 0.0)
        n_chunks = n_blk // SUBLANES
        jrow = _iota((SUBLANES, tq), 0)
        qblk = (q0 + _iota((SUBLANES, tq), 1)) // SEL_LEN
        cmp_out = [None] * NSA_HEADS
        imp_ts = []
        for g in range(NSA_KV_HEADS):
            s_t = _dot_nt(kcmp, qa_ref[g, :, :LANES])
            vct = vcmpt_ref[g * HEAD_DIM:(g + 1) * HEAD_DIM, :n_row]
            psum = None
            for hl in range(NSA_GROUP):
                h = g * NSA_GROUP + hl
                x = s_t[:, hl * tq:(hl + 1) * tq] + bias_c
                p = jnp.exp(x - jnp.max(x, axis=0, keepdims=True)) * keep_c
                p = p * (1.0 / jnp.maximum(jnp.sum(p, axis=0, keepdims=True), 1e-6))
                psum = p if psum is None else psum + p
                cmp_out[h] = gt_ref[N_BRANCH * h:N_BRANCH * h + 1, :] * _dot(vct, p.astype(MXU_DTYPE))
            ph, plo = _split_hi_lo(psum)
            ovt = ovt_ref[:n_blk, :n_row]
            imp_ts.append(_dot(ovt, ph) + _dot(ovt, plo))
        for h, res in enumerate(window_branch()):
            co_ref[h] = cmp_out[h] + res
        for g in range(NSA_KV_HEADS):
            imp_t = imp_ts[g]
            scores = []
            for c in range(n_chunks):
                dist = qblk - (jrow + c * SUBLANES)
                forced = (dist >= 0) & (dist < N_LOCAL)
                if c == 0:
                    forced = forced | (jrow == 0)
                imp_c = imp_t[c * SUBLANES:(c + 1) * SUBLANES, :]
                scores.append(jnp.where(dist < 0, -1.0, jnp.where(forced, FORCED_SCORE, imp_c)))
            ranks = [jnp.zeros((SUBLANES, tq), F32) for _ in range(n_chunks)]
            for i2 in range(min(n_blk, n_sel)):
                c2, r2 = divmod(i2, SUBLANES)
                r = jnp.broadcast_to(scores[c2][r2:r2 + 1, :], (SUBLANES, tq))
                for c in range(n_chunks):
                    if c < c2:
                        beats = r > scores[c]
                    elif c > c2:
                        beats = r >= scores[c]
                    else:
                        beats = (r > scores[c]) | ((r == scores[c]) & (jrow > r2))
                    ranks[c] = ranks[c] + jnp.where(beats, 1.0, 0.0)
            nsel = [jnp.where((ranks[c] < top) & (scores[c] >= 0.0), 0.0, NEG) for c in range(n_chunks)]
            if n_blk < nsp:
                nsel.append(jnp.full((nsp - n_blk, tq), NEG, F32))
            if nsp < LANES:
                nsel.append(jnp.zeros((LANES - nsp, tq), F32))
            nsel_q = jnp.concatenate(nsel, axis=0).T.astype(MXU_DTYPE)
            for p in range(NSA_GROUP):
                qa_ref[g, p * tq:(p + 1) * tq, LANES:] = nsel_q

    cls_blk = 2 * SUBLANES
    n_cls = -(-nsp // cls_blk)
    cls = jnp.minimum((2 * i + 1) // cls_blk, n_cls - 1)
    for k in range(n_cls):
        n_blk = min(cls_blk * (k + 1), nsp)
        n_row = min(-(-n_blk * (SEL_LEN // CMP_STRIDE) // LANES) * LANES, ncmp)

        @pl.when(cls == k)
        def _():
            compress_select(n_blk, n_row)

    t_max = eselt_ref.shape[0] // wide - 1

    def tile_start(t):
        return pl.multiple_of(jnp.minimum(t, t_max) * wide, wide)

    def sel_scores(t, slot):
        k0 = tile_start(t)
        kaug = jnp.concatenate([ks_ref[pl.ds(k0, wide), :], eselt_ref[pl.ds(k0, wide), :]], axis=1)
        for g in range(NSA_KV_HEADS):
            s_ref[slot, g] = _dot_nt(kaug, qa_ref[g])

    def sel_flash(state, t, slot, causal):
        bias = None
        if causal:
            bias = jnp.where(_iota((wide, tq), 0) - _iota((wide, tq), 1) <= q0 - t * wide, 0.0, NEG)
        s_ts = [s_ref[slot, g] for g in range(NSA_KV_HEADS)]
        return flash(state, s_ts, bias, vst_ref, tile_start(t), wide)

    def sel_body(u, state):
        sel_scores(2 * u + 1, 1)
        state = sel_flash(state, 2 * u, 0, False)
        sel_scores(2 * u + 2, 0)
        return sel_flash(state, 2 * u + 1, 1, False)

    n_tiles = i // 2 + 1
    n_even = n_tiles + n_tiles % 2
    sel_scores(0, 0)
    sel_state = lax.fori_loop(0, n_even // 2 - 1, sel_body, fresh_state())
    sel_scores(n_even - 1, 1)
    sel_state = sel_flash(sel_state, n_even - 2, 0, True)

    sel_state = sel_flash(sel_state, n_even - 1, 1, True)
    outs = [co_ref[h] + res for h, res in enumerate(gated(sel_state, 1))]

    for hl in range(NSA_GROUP):
        pair_t = jnp.concatenate([outs[hl], outs[NSA_GROUP + hl]], axis=0)
        o_ref[:, hl * LANES:(hl + 1) * LANES] = pair_t.T


def _nsa_attention(nq, gates_t, kcmp, vcmp_t, ks, vs_t, kw, vw_t, tq):
    B, T, _ = nq.shape
    ncmp = kcmp.shape[1]
    n_sel = T // SEL_LEN
    nsp = -(-n_sel // SUBLANES) * SUBLANES
    cs = jnp.arange(ncmp) * CMP_STRIDE
    ss = jnp.arange(nsp) * SEL_LEN
    ov = jnp.clip(jnp.minimum(cs[None, :] + CMP_LEN, ss[:, None] + SEL_LEN)
                  - jnp.maximum(cs[None, :], ss[:, None]), 0, None).astype(F32) / CMP_LEN
    ov = jnp.where((jnp.arange(nsp) < n_sel)[:, None] & (jnp.arange(ncmp) < ncmp - 1)[None, :], ov, 0.0)
    ovt = ov.astype(MXU_DTYPE)
    esel_t = ((jnp.arange(T) // SEL_LEN)[:, None] == jnp.arange(LANES)[None, :]).astype(MXU_DTYPE)
    kern = functools.partial(_nsa_kernel, tq=tq, n_sel=n_sel, nsp=nsp)
    per_b = lambda r, w: pl.BlockSpec((None, r, w), lambda b, i: (b, 0, 0))
    return pl.pallas_call(
        kern,
        out_shape=jax.ShapeDtypeStruct((B, T, NSA_WIDTH), F32),
        grid=(B, T // tq),
        in_specs=[pl.BlockSpec((None, tq, NSA_WIDTH), lambda b, i: (b, i, 0)),
                  pl.BlockSpec((None, LANES, tq), lambda b, i: (b, 0, i)),
                  per_b(ncmp, LANES), per_b(LANES, ncmp),
                  per_b(T, LANES), per_b(LANES, T), per_b(T, LANES), per_b(LANES, T),
                  pl.BlockSpec((nsp, ncmp), lambda b, i: (0, 0)),
                  pl.BlockSpec((T, LANES), lambda b, i: (0, 0))],
        out_specs=pl.BlockSpec((None, tq, NSA_WIDTH), lambda b, i: (b, i, 0)),
        scratch_shapes=[pltpu.VMEM((NSA_KV_HEADS, NSA_GROUP * tq, 2 * LANES), MXU_DTYPE),
                        pltpu.VMEM((2, NSA_KV_HEADS, 2 * tq, NSA_GROUP * tq), F32),
                        pltpu.VMEM((NSA_HEADS, HEAD_DIM, tq), F32)],
        compiler_params=_cparams(("parallel", "parallel")),
        name="nsa_attention",
    )(nq, gates_t, kcmp, vcmp_t, ks, vs_t, kw, vw_t, ovt, esel_t)


def _rms(v, g):
    return (v * lax.rsqrt(jnp.mean(v * v, axis=-1, keepdims=True) + EPS)) * g


def _ffn_kernel(x_ref, osb_ref, onsa_ref, mod_ref, gsb_ref, gnsa_ref, wo_ref, g_ref, win_ref, cw_ref,
                wdn_ref, fg_ref, o_ref, prev_ref, ubuf_ref, z_ref, *, tm, fc, final_norm):
    t = pl.program_id(1)

    @pl.when(t == 0)
    def _():
        prev_ref[...] = jnp.zeros_like(prev_ref)

    a_sb = _rms(osb_ref[...], gsb_ref[...]).astype(MXU_DTYPE)
    a_nsa = _rms(onsa_ref[...], gnsa_ref[...]).astype(MXU_DTYPE)
    attn = _dot(a_sb, wo_ref[:SB_WIDTH, :]) + _dot(a_nsa, wo_ref[SB_WIDTH:, :])
    x = x_ref[...] + mod_ref[2:3, :] * attn
    h = _rms(x, g_ref[...]) * (1.0 + mod_ref[4:5, :]) + mod_ref[3:4, :]
    hb = h.astype(MXU_DTYPE)
    def conv(u, c0, slot):
        ubuf_ref[slot, :SUBLANES, :] = prev_ref[:, c0:c0 + fc]
        ubuf_ref[slot, SUBLANES:, :] = u
        prev_ref[:, c0:c0 + fc] = u[tm - SUBLANES:, :]
        u1 = ubuf_ref[slot, SUBLANES - 1:SUBLANES - 1 + tm, :]
        u2 = ubuf_ref[slot, SUBLANES - 2:SUBLANES - 2 + tm, :]
        cw = cw_ref[:, c0:c0 + fc]
        return cw[2:3, :] * u + cw[1:2, :] * u1 + cw[0:1, :] * u2 + cw[3:4, :]

    def up(c):
        return (_dot(hb, win_ref[:, c * fc:(c + 1) * fc]),
                _dot(hb, win_ref[:, D_FF + c * fc:D_FF + (c + 1) * fc]))

    n_chunks = D_FF // fc
    ua, ub = up(0)
    for c in range(n_chunks):
        nxt = up(c + 1) if c + 1 < n_chunks else None
        ya = conv(ua, c * fc, 2 * (c % 2))
        yb = conv(ub, D_FF + c * fc, 2 * (c % 2) + 1)
        z_ref[:, c * fc:(c + 1) * fc] = ((ya * jax.nn.sigmoid(ya)) * yb).astype(MXU_DTYPE)
        if nxt is not None:
            ua, ub = nxt
    y = x + mod_ref[5:6, :] * _dot(z_ref[...], wdn_ref[...])
    if final_norm:
        y = _rms(y, fg_ref[...])
    o_ref[...] = y


def _outproj_ffn(x, o_sb, o_nsa, mod_l, g_sb, g_nsa, w_out, ln_g, w_in, conv_wb, w_down, final_g,
                 tm, fc, final_norm):
    B, T, D = x.shape
    row = lambda w: pl.BlockSpec((None, tm, w), lambda b, t: (b, t, 0))
    vec = lambda w: pl.BlockSpec((1, w), lambda b, t: (0, 0))
    const = lambda r, c: pl.BlockSpec((r, c), lambda b, t: (0, 0), pipeline_mode=pl.Buffered(1))
    kern = functools.partial(_ffn_kernel, tm=tm, fc=fc, final_norm=final_norm)
    return pl.pallas_call(
        kern,
        out_shape=jax.ShapeDtypeStruct((B, T, D), F32),
        grid=(B, T // tm),
        in_specs=[row(D), row(SB_WIDTH), row(NSA_WIDTH),
                  pl.BlockSpec((None, 6, D), lambda b, t: (b, 0, 0)),
                  vec(SB_WIDTH), vec(NSA_WIDTH), const(D, D), vec(D),
                  const(D, 2 * D_FF), const(SUBLANES, 2 * D_FF), const(D_FF, D), vec(D)],
        out_specs=row(D),
        scratch_shapes=[pltpu.VMEM((SUBLANES, 2 * D_FF), F32),
                        pltpu.VMEM((4, SUBLANES + tm, fc), F32),
                        pltpu.VMEM((tm, D_FF), MXU_DTYPE)],
        compiler_params=_cparams(("parallel", "arbitrary")),
        name="outproj_ffn",
    )(x, o_sb, o_nsa, mod_l, g_sb, g_nsa, w_out, ln_g, w_in, conv_wb, w_down, final_g)


_NSA_HEAD_ORDER = [h for p in range(NSA_GROUP) for h in (p, NSA_GROUP + p)]


def _head_cols(order):
    return jnp.asarray([h * HEAD_DIM + d for h in order for d in range(HEAD_DIM)], jnp.int32)


def _prep_w_in(w):
    nq0 = 3 * SB_WIDTH
    kv0 = nq0 + NSA_WIDTH
    gl0 = kv0 + 2 * N_BRANCH * KV_WIDTH
    nq = w[:, nq0:kv0][:, _head_cols(_NSA_HEAD_ORDER)]
    gl = jnp.pad(w[:, gl0:], ((0, 0), (0, LANES - (w.shape[1] - gl0))))
    return jnp.concatenate([w[:, :nq0], nq, w[:, kv0:gl0], gl], axis=1).astype(MXU_DTYPE)


def kernel(x, c, ln1_g, ln2_g, w_ada, b_ada, w_in, cmp_pos_k, cmp_w1_k, cmp_w2_k, cmp_pos_v, cmp_w1_v, cmp_w2_v, sb_out_g, nsa_out_g, w_out, ffn_w_in, ffn_conv_w, ffn_conv_b, ffn_w_down, final_g):
    B, T, D = x.shape
    L = w_in.shape[0]
    assert D == D_MODEL and T % 512 == 0 and T // SEL_LEN <= LANES
    tm_proj = 512
    tm_ffn = 256
    fc = 256
    tq = 128
    nch = T // CMP_STRIDE

    half = HEAD_DIM // 2
    inv = ROPE_THETA ** (-jnp.arange(half, dtype=F32) / half)
    inv_lanes = jnp.tile(inv, LANES // half).reshape(1, LANES)
    cos_t, sin_t = _rope_tables(inv_lanes, T, 1, 0)
    cos_c, sin_c = _rope_tables(inv_lanes, nch, CMP_STRIDE, CMP_LEN - 1)

    mod = _modulation(c, w_ada, b_ada).reshape(L, B, 6, D)
    nsa_cols = _head_cols(_NSA_HEAD_ORDER)

    for l in range(L):
        mod_l = mod[l]
        (sbq, sbk, sbv, nq, kc, vc, ks, vs, kw, vw, gates) = _projection(
            x, mod_l, ln1_g[l].reshape(1, D), _prep_w_in(w_in[l]), cos_t, sin_t, tm_proj)
        o_sb = _sb_attention(sbq, sbk, sbv, tq, 2 * tq)
        kcmp, vcmp = _compress(kc, vc, cmp_w1_k[l], cmp_w2_k[l], cmp_pos_k[l],
                               cmp_w1_v[l], cmp_w2_v[l], cmp_pos_v[l], cos_c, sin_c)
        o_nsa = _nsa_attention(nq, gates, kcmp, vcmp, ks, vs, kw, vw, tq)
        w_o = jnp.concatenate([w_out[l][:SB_WIDTH], w_out[l][SB_WIDTH:][nsa_cols]], axis=0).astype(MXU_DTYPE)
        conv_wb = jnp.concatenate([ffn_conv_w[l], ffn_conv_b[l][None, :],
                                   jnp.zeros((SUBLANES - CONV_W - 1, 2 * D_FF), F32)], axis=0)
        x = _outproj_ffn(x, o_sb, o_nsa, mod_l, sb_out_g[l].reshape(1, SB_WIDTH),
                         nsa_out_g[l][nsa_cols].reshape(1, NSA_WIDTH), w_o,
                         ln2_g[l].reshape(1, D), ffn_w_in[l].astype(MXU_DTYPE), conv_wb,
                         ffn_w_down[l].astype(MXU_DTYPE), final_g.reshape(1, D), tm_ffn, fc, l == L - 1)
    return x
```

```python
import functools
import math

import jax
import jax.numpy as jnp
from jax import lax
from jax.experimental import pallas as pl
from jax.experimental.pallas import tpu as pltpu

D_MODEL = 1024
HEAD_DIM = 64
SB_HEADS = 8
NSA_HEADS = 8
NSA_KV_HEADS = 2
NSA_GROUP = NSA_HEADS // NSA_KV_HEADS
SB_WIDTH = SB_HEADS * HEAD_DIM
NSA_WIDTH = NSA_HEADS * HEAD_DIM
KV_WIDTH = NSA_KV_HEADS * HEAD_DIM
N_BRANCH = 3
CMP_LEN = 32
CMP_STRIDE = 16
CMP_HIDDEN = 256
SEL_LEN = 64
SEL_TOPK = 16
N_LOCAL = 2
WINDOW = 512
ROPE_THETA = 10000.0
D_FF = 2816
CONV_W = 3
EPS = 1e-6
NEG = -1e30
FORCED_SCORE = 1e6

F32 = jnp.float32
MXU_DTYPE = jnp.bfloat16
LANES = 128
SUBLANES = 8
VMEM_LIMIT = 56 * 1024 * 1024
QK_SCALE = HEAD_DIM ** -0.5
EXP_ZERO_BELOW = -104.0

_NT = (((1,), (1,)), ((), ()))


def _cparams(sem):
    return pltpu.CompilerParams(dimension_semantics=sem, vmem_limit_bytes=VMEM_LIMIT)


def _dot(a, b):
    return jnp.dot(a, b, preferred_element_type=F32)


def _dot_nt(a, b):
    return lax.dot_general(a, b, _NT, preferred_element_type=F32)


def _split_hi_lo(a):
    hi = a.astype(MXU_DTYPE)
    lo = (a - hi.astype(F32)).astype(MXU_DTYPE)
    return hi, lo


def _iota(shape, dim):
    return lax.broadcasted_iota(jnp.int32, shape, dim)


def _mod_kernel(c_ref, w_ref, b_ref, o_ref):
    c = c_ref[...]
    a = c * jax.nn.sigmoid(c)
    o_ref[...] = jnp.dot(a, w_ref[...], preferred_element_type=F32,
                         precision=lax.Precision.HIGHEST) + b_ref[...]


def _modulation(c, w_ada, b_ada):
    L, D, D6 = w_ada.shape
    B = c.shape[0]
    n = D6 // D
    return pl.pallas_call(
        _mod_kernel,
        out_shape=jax.ShapeDtypeStruct((L, B, D6), F32),
        grid=(L, n),
        in_specs=[pl.BlockSpec((B, D), lambda l, j: (0, 0)),
                  pl.BlockSpec((None, D, D), lambda l, j: (l, 0, j)),
                  pl.BlockSpec((None, 1, D), lambda l, j: (l, 0, j))],
        out_specs=pl.BlockSpec((None, B, D), lambda l, j: (l, 0, j)),
        compiler_params=_cparams(("parallel", "parallel")),
        name="adaln_mod",
    )(c, w_ada, b_ada.reshape(L, 1, D6))


def _rope_table_kernel(inv_ref, cos_ref, sin_ref, *, rows, stride, offset):
    i = pl.program_id(0)
    r = _iota((rows, LANES), 0) + i * rows
    pos = (r * stride + offset).astype(F32)
    ang = pos * inv_ref[...]
    lane = _iota((rows, LANES), 1)
    first = (lane % HEAD_DIM) < (HEAD_DIM // 2)
    s = jnp.sin(ang)
    cos_ref[...] = jnp.cos(ang)
    sin_ref[...] = jnp.where(first, -s, s)


def _rope_tables(inv_lanes, n_rows, stride, offset):
    rows = min(n_rows, 512)
    kern = functools.partial(_rope_table_kernel, rows=rows, stride=stride, offset=offset)
    return pl.pallas_call(
        kern,
        out_shape=(jax.ShapeDtypeStruct((n_rows, LANES), F32),) * 2,
        grid=(n_rows // rows,),
        in_specs=[pl.BlockSpec((1, LANES), lambda i: (0, 0))],
        out_specs=(pl.BlockSpec((rows, LANES), lambda i: (i, 0)),) * 2,
        compiler_params=_cparams(("parallel",)),
        name="rope_tables",
    )(inv_lanes)


def _apply_rope(v, cos, sin_signed):
    lane = _iota(v.shape, 1)
    first = (lane % HEAD_DIM) < (HEAD_DIM // 2)
    half = HEAD_DIM // 2
    partner = jnp.where(first, pltpu.roll(v, LANES - half, 1), pltpu.roll(v, half, 1))
    return v * cos + partner * sin_signed


_PROJ_COLS = 3 * SB_WIDTH + NSA_WIDTH + 2 * N_BRANCH * KV_WIDTH + LANES


def _proj_kernel(x_ref, mod_ref, g_ref, w_ref, cos_ref, sin_ref,
                 sbq_ref, sbk_ref, sbv_ref, nq_ref, kc_ref, vc_ref,
                 ks_ref, vs_ref, kw_ref, vw_ref, gate_ref):
    x = x_ref[...]
    shift = mod_ref[0:1, :]
    scale = mod_ref[1:2, :]
    ms = jnp.mean(x * x, axis=-1, keepdims=True)
    h = (x * lax.rsqrt(ms + EPS)) * g_ref[...]
    h = h * (1.0 + scale) + shift
    hb = h.astype(MXU_DTYPE)
    cos = cos_ref[...]
    sin = sin_ref[...]

    def mm(lo, width):
        return _dot(hb, w_ref[:, lo:lo + width])

    o = 0
    sbq_ref[...] = (mm(o, SB_WIDTH) * QK_SCALE).astype(sbq_ref.dtype); o += SB_WIDTH
    sbk_ref[...] = mm(o, SB_WIDTH).astype(sbk_ref.dtype); o += SB_WIDTH
    sbv_ref[...] = mm(o, SB_WIDTH).astype(sbv_ref.dtype); o += SB_WIDTH
    nq = mm(o, NSA_WIDTH); o += NSA_WIDTH
    for p in range(NSA_WIDTH // LANES):
        blk = _apply_rope(nq[:, p * LANES:(p + 1) * LANES], cos, sin) * QK_SCALE
        nq_ref[:, p * LANES:(p + 1) * LANES] = blk.astype(nq_ref.dtype)
    rest = mm(o, 2 * N_BRANCH * KV_WIDTH + LANES)
    kc_ref[...] = rest[:, 0 * LANES:1 * LANES].astype(kc_ref.dtype)
    vc_ref[...] = rest[:, 1 * LANES:2 * LANES].astype(vc_ref.dtype)
    ks_ref[...] = _apply_rope(rest[:, 2 * LANES:3 * LANES], cos, sin).astype(ks_ref.dtype)
    kw_ref[...] = _apply_rope(rest[:, 4 * LANES:5 * LANES], cos, sin).astype(kw_ref.dtype)
    vs_ref[...] = rest[:, 3 * LANES:4 * LANES].T.astype(vs_ref.dtype)
    vw_ref[...] = rest[:, 5 * LANES:6 * LANES].T.astype(vw_ref.dtype)
    gate_ref[...] = jax.nn.sigmoid(rest[:, 6 * LANES:7 * LANES]).T


def _projection(x, mod_l, ln_g, w_cat, cos_t, sin_t, tm):
    B, T, D = x.shape
    row = lambda w: pl.BlockSpec((None, tm, w), lambda b, t: (b, t, 0))
    col = pl.BlockSpec((None, LANES, tm), lambda b, t: (b, 0, t))
    shp = lambda w, dt: jax.ShapeDtypeStruct((B, T, w), dt)
    shp_t = lambda dt: jax.ShapeDtypeStruct((B, LANES, T), dt)
    md = MXU_DTYPE
    return pl.pallas_call(
        _proj_kernel,
        out_shape=(shp(SB_WIDTH, md), shp(SB_WIDTH, md), shp(SB_WIDTH, md), shp(NSA_WIDTH, md),
                   shp(LANES, F32), shp(LANES, F32), shp(LANES, md), shp_t(md),
                   shp(LANES, md), shp_t(md), shp_t(F32)),
        grid=(B, T // tm),
        in_specs=[row(D),
                  pl.BlockSpec((None, 6, D), lambda b, t: (b, 0, 0)),
                  pl.BlockSpec((1, D), lambda b, t: (0, 0)),
                  pl.BlockSpec((D, _PROJ_COLS), lambda b, t: (0, 0)),
                  pl.BlockSpec((tm, LANES), lambda b, t: (t, 0)),
                  pl.BlockSpec((tm, LANES), lambda b, t: (t, 0))],
        out_specs=(row(SB_WIDTH), row(SB_WIDTH), row(SB_WIDTH), row(NSA_WIDTH),
                   row(LANES), row(LANES), row(LANES), col, row(LANES), col, col),
        compiler_params=_cparams(("parallel", "parallel")),
        name="norm_proj",
    )(x, mod_l, ln_g, w_cat, cos_t, sin_t)


def _sb_kernel(q_ref, k_ref, v_ref, uo_ref, o_ref, carry_ref, *, tq):
    n_pairs = q_ref.shape[1] // LANES
    low = _iota((tq, LANES), 1) < HEAD_DIM
    causal = _iota((tq, tq), 1) < _iota((tq, tq), 0)

    def sweep(tiles, rows):
        pairs = [slice(p * LANES, (p + 1) * LANES) for p in range(n_pairs)]
        heads = range(2 * n_pairs)
        resume = tiles[0][1] == "back"
        carry = [carry_ref[h] if resume else None for h in heads]
        out = [o_ref[rows, sl] if resume else None for sl in pairs]
        exists = [kb >= 0 for kb, _ in tiles]
        uo = uo_ref[...]
        starts = [pl.multiple_of(jnp.maximum(kb, 0) * tq, tq) for kb, _ in tiles]
        zs = []
        for k0 in starts:
            for sl in pairs:
                q = q_ref[rows, sl]
                kt = k_ref[pl.ds(k0, tq), sl]
                zq = jnp.zeros_like(q)
                zs.append(_dot_nt(jnp.where(low, q, zq), kt))
                zs.append(_dot_nt(jnp.where(low, zq, q), kt))
        lsns, sums = [], []
        for n, z in enumerate(zs):
            t = n // len(heads)
            mode = tiles[t][1]
            lsn = -(jnp.maximum(z, 0.0) + jnp.log(1.0 + jnp.exp(-jnp.abs(z))))
            if mode == "diag":
                lsn = jnp.where(causal, lsn, 0.0)
            elif mode == "near":
                lsn = jnp.where(exists[t], lsn, 0.0)
            hi, lo = _split_hi_lo(lsn)
            lsns.append(lsn)
            sums.append(_dot(hi, uo) + _dot(lo, uo))
        ws = []
        for n in range(len(zs)):
            t = n // len(heads)
            mode = tiles[t][1]
            h = n % len(heads)
            tail = sums[n][:, :tq]
            total = sums[n][:, tq:]
            if carry[h] is not None:
                tail = tail + carry[h]
                total = total + carry[h]
            w = jnp.exp(zs[n] + lsns[n] + tail)
            if mode == "diag":
                w = jnp.where(causal, w, 0.0)
            elif mode == "near":
                w = jnp.where(exists[t], w, 0.0)
            ws.append(w.astype(MXU_DTYPE))
            carry[h] = total
        for t, k0 in enumerate(starts):
            for p, sl in enumerate(pairs):
                vt = v_ref[pl.ds(k0, tq), sl]
                zv = jnp.zeros_like(vt)
                v2 = jnp.concatenate([jnp.where(low, vt, zv), jnp.where(low, zv, vt)], axis=0)
                n = t * len(heads) + 2 * p
                upd = _dot(jnp.concatenate(ws[n:n + 2], axis=1), v2)
                out[p] = upd if out[p] is None else out[p] + upd
        worst = carry[0]
        for h in heads:
            carry_ref[h] = carry[h]
            worst = jnp.maximum(worst, carry[h])
        for p, sl in enumerate(pairs):
            o_ref[rows, sl] = out[p]
        return (jnp.max(worst) > EXP_ZERO_BELOW).astype(jnp.int32)

    def cond(st):
        kb, live = st
        return jnp.logical_and(kb >= 0, live > 0)

    for sub in range(q_ref.shape[0] // tq):
        i = pl.program_id(1) * (q_ref.shape[0] // tq) + sub
        rows = slice(sub * tq, (sub + 1) * tq)

        def body(st):
            kb, _ = st
            return kb - 1, sweep([(kb, "back")], rows)

        first = sweep([(i, "diag"), (i - 1, "near"), (i - 2, "near")], rows)
        lax.while_loop(cond, body, (i - 3, first))


def _sb_attention(sbq, sbk, sbv, tq, rows_per_step):
    B, T, W = sbq.shape
    r = jnp.arange(tq)
    u = (r[:, None] > r[None, :]).astype(MXU_DTYPE)
    uo = jnp.concatenate([u, jnp.ones((tq, tq), MXU_DTYPE)], axis=1)
    kern = functools.partial(_sb_kernel, tq=tq)
    return pl.pallas_call(
        kern,
        out_shape=jax.ShapeDtypeStruct((B, T, W), F32),
        grid=(B, T // rows_per_step),
        in_specs=[pl.BlockSpec((None, rows_per_step, W), lambda b, i: (b, i, 0)),
                  pl.BlockSpec((None, T, W), lambda b, i: (b, 0, 0)),
                  pl.BlockSpec((None, T, W), lambda b, i: (b, 0, 0)),
                  pl.BlockSpec((tq, 2 * tq), lambda b, i: (0, 0))],
        out_specs=pl.BlockSpec((None, rows_per_step, W), lambda b, i: (b, i, 0)),
        scratch_shapes=[pltpu.VMEM((W // HEAD_DIM, tq, tq), F32)],
        compiler_params=_cparams(("parallel", "parallel")),
        name="sb_attention",
    )(sbq, sbk, sbv, uo)


def _gelu_tanh(x):
    c = math.sqrt(2.0 / math.pi)
    return 0.5 * x * (1.0 + jnp.tanh(c * (x + 0.044715 * (x * x * x))))


def _compress_kernel(kc_ref, vc_ref, wck_ref, w1k_ref, w2k_ref, pk_ref, wcv_ref, w1v_ref, w2v_ref, pv_ref,
                     cos_ref, sin_ref, ko_ref, vo_ref, *, nch):
    rowi = _iota((nch, LANES), 0)

    def one(c_ref, wc_ref, w1_ref, w2_ref, p_ref, rope):
        bias = _dot(p_ref[...], w1_ref[...])[0:1, :]
        proj = jnp.zeros((nch, 2 * NSA_KV_HEADS * CMP_HIDDEN), F32)
        for l in range(CMP_STRIDE):
            x_l = c_ref[pl.ds(l, nch, stride=CMP_STRIDE), :].astype(MXU_DTYPE)
            proj = proj + _dot(x_l, wc_ref[l])
        acc = jnp.zeros((nch, LANES), F32)
        for g in range(NSA_KV_HEADS):
            a = proj[:, (2 * g) * CMP_HIDDEN:(2 * g + 1) * CMP_HIDDEN]
            b = proj[:, (2 * g + 1) * CMP_HIDDEN:(2 * g + 2) * CMP_HIDDEN]
            hid = a + pltpu.roll(b, nch - 1, 0) + bias
            acc = acc + _dot(_gelu_tanh(hid).astype(MXU_DTYPE), w2_ref[g])
        if rope:
            acc = _apply_rope(acc, cos_ref[...], sin_ref[...])
        return jnp.where(rowi < nch - 1, acc, 0.0)

    ko_ref[...] = one(kc_ref, wck_ref, w1k_ref, w2k_ref, pk_ref, True).astype(ko_ref.dtype)
    vo_ref[...] = one(vc_ref, wcv_ref, w1v_ref, w2v_ref, pv_ref, False).T.astype(vo_ref.dtype)


def _pad_w2(w2):
    z = jnp.zeros_like(w2)
    return jnp.stack([jnp.concatenate([w2, z], 1), jnp.concatenate([z, w2], 1)]).astype(MXU_DTYPE)


def _chunk_weights(w1):
    half = CMP_STRIDE * HEAD_DIM
    top = w1[:half].reshape(CMP_STRIDE, HEAD_DIM, CMP_HIDDEN)
    bot = w1[half:].reshape(CMP_STRIDE, HEAD_DIM, CMP_HIDDEN)
    tb = jnp.concatenate([top, bot], axis=2)
    z = jnp.zeros_like(tb)
    return jnp.concatenate([jnp.concatenate([tb, z], axis=2),
                            jnp.concatenate([z, tb], axis=2)], axis=1).astype(MXU_DTYPE)


def _compress(kc, vc, w1k, w2k, pk, w1v, w2v, pv, cos_c, sin_c):
    B, T, _ = kc.shape
    nch = T // CMP_STRIDE
    flat = CMP_STRIDE * HEAD_DIM
    posf = lambda p: jnp.broadcast_to(p.reshape(1, CMP_LEN * HEAD_DIM), (SUBLANES, CMP_LEN * HEAD_DIM)).astype(MXU_DTYPE)
    full = lambda *s: pl.BlockSpec(s, lambda b: (0,) * len(s))
    wide = 2 * NSA_KV_HEADS * CMP_HIDDEN
    kern = functools.partial(_compress_kernel, nch=nch)
    return pl.pallas_call(
        kern,
        out_shape=(jax.ShapeDtypeStruct((B, nch, LANES), MXU_DTYPE),
                   jax.ShapeDtypeStruct((B, LANES, nch), MXU_DTYPE)),
        grid=(B,),
        in_specs=[pl.BlockSpec((None, T, LANES), lambda b: (b, 0, 0)),
                  pl.BlockSpec((None, T, LANES), lambda b: (b, 0, 0)),
                  full(CMP_STRIDE, LANES, wide), full(2 * flat, CMP_HIDDEN), full(2, CMP_HIDDEN, LANES),
                  full(SUBLANES, 2 * flat),
                  full(CMP_STRIDE, LANES, wide), full(2 * flat, CMP_HIDDEN), full(2, CMP_HIDDEN, LANES),
                  full(SUBLANES, 2 * flat),
                  full(nch, LANES), full(nch, LANES)],
        out_specs=(pl.BlockSpec((None, nch, LANES), lambda b: (b, 0, 0)),
                   pl.BlockSpec((None, LANES, nch), lambda b: (b, 0, 0))),
        compiler_params=_cparams(("parallel",)),
        name="nsa_compress",
    )(kc, vc, _chunk_weights(w1k), w1k.astype(MXU_DTYPE), _pad_w2(w2k), posf(pk),
      _chunk_weights(w1v), w1v.astype(MXU_DTYPE), _pad_w2(w2v), posf(pv), cos_c, sin_c)


def _nsa_kernel(q_ref, gt_ref, kcmp_ref, vcmpt_ref, ks_ref, vst_ref, kw_ref, vwt_ref,
                ovt_ref, eselt_ref, o_ref, qa_ref, s_ref, co_ref, *, tq, n_sel, nsp):
    i = pl.program_id(1)
    q0 = i * tq
    ncmp = kcmp_ref.shape[0]
    top = min(SEL_TOPK, n_sel)
    wide = 2 * tq
    heads = range(NSA_HEADS)
    lane = _iota((tq, LANES), 1)

    for g in range(NSA_KV_HEADS):
        in_group = (lane // HEAD_DIM) == g
        for p in range(NSA_GROUP):
            blk = q_ref[:, p * LANES:(p + 1) * LANES]
            qa_ref[g, p * tq:(p + 1) * tq, :LANES] = jnp.where(in_group, blk, jnp.zeros_like(blk))

    def fresh_state():
        return [(jnp.full((1, tq), NEG, F32), jnp.zeros((HEAD_DIM + SUBLANES, tq), F32)) for _ in heads]

    def per_head(s_ts):
        return [s_ts[g][:, hl * tq:(hl + 1) * tq] for g in range(NSA_KV_HEADS) for hl in range(NSA_GROUP)]

    def flash(state, xs, bias, vt_ref, k0, w):
        new = []
        for g in range(NSA_KV_HEADS):
            vt = jnp.concatenate([vt_ref[g * HEAD_DIM:(g + 1) * HEAD_DIM, pl.ds(k0, w)],
                                  jnp.ones((SUBLANES, w), MXU_DTYPE)], axis=0)
            for hl in range(NSA_GROUP):
                m_old, acc_old = state[g * NSA_GROUP + hl]
                x = xs[g * NSA_GROUP + hl]
                if bias is not None:
                    x = x + bias
                m_new = jnp.maximum(m_old, jnp.max(x, axis=0, keepdims=True))
                p = jnp.exp(x - m_new).astype(MXU_DTYPE)
                new.append((m_new, jnp.exp(m_old - m_new) * acc_old + _dot(vt, p)))
        return new

    def gated(state, branch):
        res = []
        for h in heads:
            acc = state[h][1]
            r = N_BRANCH * h + branch
            scale = gt_ref[r:r + 1, :] * (1.0 / jnp.maximum(acc[HEAD_DIM:HEAD_DIM + 1, :], 1e-6))
            res.append(acc[:HEAD_DIM, :] * scale)
        return res

    n_win = WINDOW // tq

    def win_scores(j):
        k0 = pl.multiple_of(jnp.maximum(q0 - (n_win - j) * tq, 0), tq)
        kt = kw_ref[pl.ds(k0, tq), :]
        return [_dot_nt(kt, qa_ref[g, :, :LANES]) for g in range(NSA_KV_HEADS)], k0

    def window_branch():
        krow = _iota((tq, tq), 0)
        qcol = _iota((tq, tq), 1)
        causal_bias = jnp.where(krow <= qcol, 0.0, NEG)
        oldest_bias = jnp.where(krow > qcol, 0.0, NEG)
        w_scores, w_k0 = win_scores(0)
        win_state = fresh_state()
        for j in range(n_win + 1):
            nxt = win_scores(j + 1) if j < n_win else None
            bias = oldest_bias if j == 0 else (causal_bias if j == n_win else None)
            new_state = flash(win_state, per_head(w_scores), bias, vwt_ref, w_k0, tq)
            if j < n_win:
                live = i >= n_win - j
                new_state = [(jnp.where(live, mn, mo), jnp.where(live, an, ao))
                             for (mn, an), (mo, ao) in zip(new_state, win_state)]
                w_scores, w_k0 = nxt
            win_state = new_state
        return gated(win_state, 2)

    def compress_select(n_blk, n_row):
        kcmp = kcmp_ref[:n_row, :]
        valid_c = (_iota((n_row, tq), 0) * CMP_STRIDE + (CMP_LEN - 1)) <= (q0 + _iota((n_row, tq), 1))
        bias_c = jnp.where(valid_c, 0.0, NEG)
        keep_c = jnp.where(valid_c, 1.0, 0.0)
        n_chunks = n_blk // SUBLANES
        jrow = _iota((SUBLANES, tq), 0)
        qblk = (q0 + _iota((SUBLANES, tq), 1)) // SEL_LEN
        cmp_out = [None] * NSA_HEADS
        imp_ts = []
        for g in range(NSA_KV_HEADS):
            s_t = _dot_nt(kcmp, qa_ref[g, :, :LANES])
            vct = vcmpt_ref[g * HEAD_DIM:(g + 1) * HEAD_DIM, :n_row]
            psum = None
            for hl in range(NSA_GROUP):
                h = g * NSA_GROUP + hl
                x = s_t[:, hl * tq:(hl + 1) * tq] + bias_c
                p = jnp.exp(x - jnp.max(x, axis=0, keepdims=True)) * keep_c
                p = p * (1.0 / jnp.maximum(jnp.sum(p, axis=0, keepdims=True), 1e-6))
                psum = p if psum is None else psum + p
                cmp_out[h] = gt_ref[N_BRANCH * h:N_BRANCH * h + 1, :] * _dot(vct, p.astype(MXU_DTYPE))
            ph, plo = _split_hi_lo(psum)
            ovt = ovt_ref[:n_blk, :n_row]
            imp_ts.append(_dot(ovt, ph) + _dot(ovt, plo))
        for h, res in enumerate(window_branch()):
            co_ref[h] = cmp_out[h] + res
        for g in range(NSA_KV_HEADS):
            imp_t = imp_ts[g]
            scores = []
            for c in range(n_chunks):
                dist = qblk - (jrow + c * SUBLANES)
                forced = (dist >= 0) & (dist < N_LOCAL)
                if c == 0:
                    forced = forced | (jrow == 0)
                imp_c = imp_t[c * SUBLANES:(c + 1) * SUBLANES, :]
                scores.append(jnp.where(dist < 0, -1.0, jnp.where(forced, FORCED_SCORE, imp_c)))
            ranks = [jnp.zeros((SUBLANES, tq), F32) for _ in range(n_chunks)]
            for i2 in range(min(n_blk, n_sel)):
                c2, r2 = divmod(i2, SUBLANES)
                r = jnp.broadcast_to(scores[c2][r2:r2 + 1, :], (SUBLANES, tq))
                for c in range(n_chunks):
                    if c < c2:
                        beats = r > scores[c]
                    elif c > c2:
                        beats = r >= scores[c]
                    else:
                        beats = (r > scores[c]) | ((r == scores[c]) & (jrow > r2))
                    ranks[c] = ranks[c] + jnp.where(beats, 1.0, 0.0)
            nsel = [jnp.where((ranks[c] < top) & (scores[c] >= 0.0), 0.0, NEG) for c in range(n_chunks)]
            if n_blk < nsp:
                nsel.append(jnp.full((nsp - n_blk, tq), NEG, F32))
            if nsp < LANES:
                nsel.append(jnp.zeros((LANES - nsp, tq), F32))
            nsel_q = jnp.concatenate(nsel, axis=0).T.astype(MXU_DTYPE)
            for p in range(NSA_GROUP):
                qa_ref[g, p * tq:(p + 1) * tq, LANES:] = nsel_q

    cls_blk = 2 * SUBLANES
    n_cls = -(-nsp // cls_blk)
    cls = jnp.minimum((2 * i + 1) // cls_blk, n_cls - 1)
    for k in range(n_cls):
        n_blk = min(cls_blk * (k + 1), nsp)
        n_row = min(-(-n_blk * (SEL_LEN // CMP_STRIDE) // LANES) * LANES, ncmp)

        @pl.when(cls == k)
        def _():
            compress_select(n_blk, n_row)

    t_max = eselt_ref.shape[0] // wide - 1

    def tile_start(t):
        return pl.multiple_of(jnp.minimum(t, t_max) * wide, wide)

    def sel_scores(t, slot):
        k0 = tile_start(t)
        kaug = jnp.concatenate([ks_ref[pl.ds(k0, wide), :], eselt_ref[pl.ds(k0, wide), :]], axis=1)
        s_ts = [_dot_nt(kaug, qa_ref[g]) for g in range(NSA_KV_HEADS)]
        for h, x in enumerate(per_head(s_ts)):
            s_ref[slot, h] = x

    def sel_flash(state, t, slot, causal):
        bias = None
        if causal:
            bias = jnp.where(_iota((wide, tq), 0) - _iota((wide, tq), 1) <= q0 - t * wide, 0.0, NEG)
        xs = [s_ref[slot, h] for h in heads]
        return flash(state, xs, bias, vst_ref, tile_start(t), wide)

    def sel_body(u, state):
        sel_scores(2 * u + 1, 1)
        state = sel_flash(state, 2 * u, 0, False)
        sel_scores(2 * u + 2, 0)
        return sel_flash(state, 2 * u + 1, 1, False)

    n_tiles = i // 2 + 1
    n_even = n_tiles + n_tiles % 2
    sel_scores(0, 0)
    sel_state = lax.fori_loop(0, n_even // 2 - 1, sel_body, fresh_state())
    sel_scores(n_even - 1, 1)
    sel_state = sel_flash(sel_state, n_even - 2, 0, True)

    sel_state = sel_flash(sel_state, n_even - 1, 1, True)
    outs = [co_ref[h] + res for h, res in enumerate(gated(sel_state, 1))]

    for hl in range(NSA_GROUP):
        pair_t = jnp.concatenate([outs[hl], outs[NSA_GROUP + hl]], axis=0)
        o_ref[:, hl * LANES:(hl + 1) * LANES] = pair_t.T


def _nsa_attention(nq, gates_t, kcmp, vcmp_t, ks, vs_t, kw, vw_t, tq):
    B, T, _ = nq.shape
    ncmp = kcmp.shape[1]
    n_sel = T // SEL_LEN
    nsp = -(-n_sel // SUBLANES) * SUBLANES
    cs = jnp.arange(ncmp) * CMP_STRIDE
    ss = jnp.arange(nsp) * SEL_LEN
    ov = jnp.clip(jnp.minimum(cs[None, :] + CMP_LEN, ss[:, None] + SEL_LEN)
                  - jnp.maximum(cs[None, :], ss[:, None]), 0, None).astype(F32) / CMP_LEN
    ov = jnp.where((jnp.arange(nsp) < n_sel)[:, None] & (jnp.arange(ncmp) < ncmp - 1)[None, :], ov, 0.0)
    ovt = ov.astype(MXU_DTYPE)
    esel_t = ((jnp.arange(T) // SEL_LEN)[:, None] == jnp.arange(LANES)[None, :]).astype(MXU_DTYPE)
    kern = functools.partial(_nsa_kernel, tq=tq, n_sel=n_sel, nsp=nsp)
    per_b = lambda r, w: pl.BlockSpec((None, r, w), lambda b, i: (b, 0, 0))
    return pl.pallas_call(
        kern,
        out_shape=jax.ShapeDtypeStruct((B, T, NSA_WIDTH), F32),
        grid=(B, T // tq),
        in_specs=[pl.BlockSpec((None, tq, NSA_WIDTH), lambda b, i: (b, i, 0)),
                  pl.BlockSpec((None, LANES, tq), lambda b, i: (b, 0, i)),
                  per_b(ncmp, LANES), per_b(LANES, ncmp),
                  per_b(T, LANES), per_b(LANES, T), per_b(T, LANES), per_b(LANES, T),
                  pl.BlockSpec((nsp, ncmp), lambda b, i: (0, 0)),
                  pl.BlockSpec((T, LANES), lambda b, i: (0, 0))],
        out_specs=pl.BlockSpec((None, tq, NSA_WIDTH), lambda b, i: (b, i, 0)),
        scratch_shapes=[pltpu.VMEM((NSA_KV_HEADS, NSA_GROUP * tq, 2 * LANES), MXU_DTYPE),
                        pltpu.VMEM((2, NSA_HEADS, 2 * tq, tq), F32),
                        pltpu.VMEM((NSA_HEADS, HEAD_DIM, tq), F32)],
        compiler_params=_cparams(("parallel", "parallel")),
        name="nsa_attention",
    )(nq, gates_t, kcmp, vcmp_t, ks, vs_t, kw, vw_t, ovt, esel_t)


def _rms(v, g):
    return (v * lax.rsqrt(jnp.mean(v * v, axis=-1, keepdims=True) + EPS)) * g


def _ffn_kernel(x_ref, osb_ref, onsa_ref, mod_ref, gsb_ref, gnsa_ref, wo_ref, g_ref, win_ref, cw_ref,
                wdn_ref, fg_ref, o_ref, prev_ref, ubuf_ref, z_ref, *, tm, fc, final_norm):
    t = pl.program_id(1)

    @pl.when(t == 0)
    def _():
        prev_ref[...] = jnp.zeros_like(prev_ref)

    a_sb = _rms(osb_ref[...], gsb_ref[...]).astype(MXU_DTYPE)
    a_nsa = _rms(onsa_ref[...], gnsa_ref[...]).astype(MXU_DTYPE)
    attn = _dot(a_sb, wo_ref[:SB_WIDTH, :]) + _dot(a_nsa, wo_ref[SB_WIDTH:, :])
    x = x_ref[...] + mod_ref[2:3, :] * attn
    h = _rms(x, g_ref[...]) * (1.0 + mod_ref[4:5, :]) + mod_ref[3:4, :]
    hb = h.astype(MXU_DTYPE)
    def conv(u, c0, slot):
        ubuf_ref[slot, :SUBLANES, :] = prev_ref[:, c0:c0 + fc]
        ubuf_ref[slot, SUBLANES:, :] = u
        prev_ref[:, c0:c0 + fc] = u[tm - SUBLANES:, :]
        u1 = ubuf_ref[slot, SUBLANES - 1:SUBLANES - 1 + tm, :]
        u2 = ubuf_ref[slot, SUBLANES - 2:SUBLANES - 2 + tm, :]
        cw = cw_ref[:, c0:c0 + fc]
        return cw[2:3, :] * u + cw[1:2, :] * u1 + cw[0:1, :] * u2 + cw[3:4, :]

    def up(c):
        return (_dot(hb, win_ref[:, c * fc:(c + 1) * fc]),
                _dot(hb, win_ref[:, D_FF + c * fc:D_FF + (c + 1) * fc]))

    n_chunks = D_FF // fc
    ua, ub = up(0)
    for c in range(n_chunks):
        nxt = up(c + 1) if c + 1 < n_chunks else None
        ya = conv(ua, c * fc, 2 * (c % 2))
        yb = conv(ub, D_FF + c * fc, 2 * (c % 2) + 1)
        z_ref[:, c * fc:(c + 1) * fc] = ((ya * jax.nn.sigmoid(ya)) * yb).astype(MXU_DTYPE)
        if nxt is not None:
            ua, ub = nxt
    y = x + mod_ref[5:6, :] * _dot(z_ref[...], wdn_ref[...])
    if final_norm:
        y = _rms(y, fg_ref[...])
    o_ref[...] = y


def _outproj_ffn(x, o_sb, o_nsa, mod_l, g_sb, g_nsa, w_out, ln_g, w_in, conv_wb, w_down, final_g,
                 tm, fc, final_norm):
    B, T, D = x.shape
    row = lambda w: pl.BlockSpec((None, tm, w), lambda b, t: (b, t, 0))
    vec = lambda w: pl.BlockSpec((1, w), lambda b, t: (0, 0))
    const = lambda r, c: pl.BlockSpec((r, c), lambda b, t: (0, 0), pipeline_mode=pl.Buffered(1))
    kern = functools.partial(_ffn_kernel, tm=tm, fc=fc, final_norm=final_norm)
    return pl.pallas_call(
        kern,
        out_shape=jax.ShapeDtypeStruct((B, T, D), F32),
        grid=(B, T // tm),
        in_specs=[row(D), row(SB_WIDTH), row(NSA_WIDTH),
                  pl.BlockSpec((None, 6, D), lambda b, t: (b, 0, 0)),
                  vec(SB_WIDTH), vec(NSA_WIDTH), const(D, D), vec(D),
                  const(D, 2 * D_FF), const(SUBLANES, 2 * D_FF), const(D_FF, D), vec(D)],
        out_specs=row(D),
        scratch_shapes=[pltpu.VMEM((SUBLANES, 2 * D_FF), F32),
                        pltpu.VMEM((4, SUBLANES + tm, fc), F32),
                        pltpu.VMEM((tm, D_FF), MXU_DTYPE)],
        compiler_params=_cparams(("parallel", "arbitrary")),
        name="outproj_ffn",
    )(x, o_sb, o_nsa, mod_l, g_sb, g_nsa, w_out, ln_g, w_in, conv_wb, w_down, final_g)


_NSA_HEAD_ORDER = [h for p in range(NSA_GROUP) for h in (p, NSA_GROUP + p)]


def _head_cols(order):
    return jnp.asarray([h * HEAD_DIM + d for h in order for d in range(HEAD_DIM)], jnp.int32)


def _prep_w_in(w):
    nq0 = 3 * SB_WIDTH
    kv0 = nq0 + NSA_WIDTH
    gl0 = kv0 + 2 * N_BRANCH * KV_WIDTH
    nq = w[:, nq0:kv0][:, _head_cols(_NSA_HEAD_ORDER)]
    gl = jnp.pad(w[:, gl0:], ((0, 0), (0, LANES - (w.shape[1] - gl0))))
    return jnp.concatenate([w[:, :nq0], nq, w[:, kv0:gl0], gl], axis=1).astype(MXU_DTYPE)


def kernel(x, c, ln1_g, ln2_g, w_ada, b_ada, w_in, cmp_pos_k, cmp_w1_k, cmp_w2_k, cmp_pos_v, cmp_w1_v, cmp_w2_v, sb_out_g, nsa_out_g, w_out, ffn_w_in, ffn_conv_w, ffn_conv_b, ffn_w_down, final_g):
    B, T, D = x.shape
    L = w_in.shape[0]
    assert D == D_MODEL and T % 512 == 0 and T // SEL_LEN <= LANES
    tm_proj = 512
    tm_ffn = 256
    fc = 256
    tq = 128
    nch = T // CMP_STRIDE

    half = HEAD_DIM // 2
    inv = ROPE_THETA ** (-jnp.arange(half, dtype=F32) / half)
    inv_lanes = jnp.tile(inv, LANES // half).reshape(1, LANES)
    cos_t, sin_t = _rope_tables(inv_lanes, T, 1, 0)
    cos_c, sin_c = _rope_tables(inv_lanes, nch, CMP_STRIDE, CMP_LEN - 1)

    mod = _modulation(c, w_ada, b_ada).reshape(L, B, 6, D)
    nsa_cols = _head_cols(_NSA_HEAD_ORDER)

    for l in range(L):
        mod_l = mod[l]
        (sbq, sbk, sbv, nq, kc, vc, ks, vs, kw, vw, gates) = _projection(
            x, mod_l, ln1_g[l].reshape(1, D), _prep_w_in(w_in[l]), cos_t, sin_t, tm_proj)
        o_sb = _sb_attention(sbq, sbk, sbv, tq, 2 * tq)
        kcmp, vcmp = _compress(kc, vc, cmp_w1_k[l], cmp_w2_k[l], cmp_pos_k[l],
                               cmp_w1_v[l], cmp_w2_v[l], cmp_pos_v[l], cos_c, sin_c)
        o_nsa = _nsa_attention(nq, gates, kcmp, vcmp, ks, vs, kw, vw, tq)
        w_o = jnp.concatenate([w_out[l][:SB_WIDTH], w_out[l][SB_WIDTH:][nsa_cols]], axis=0).astype(MXU_DTYPE)
        conv_wb = jnp.concatenate([ffn_conv_w[l], ffn_conv_b[l][None, :],
                                   jnp.zeros((SUBLANES - CONV_W - 1, 2 * D_FF), F32)], axis=0)
        x = _outproj_ffn(x, o_sb, o_nsa, mod_l, sb_out_g[l].reshape(1, SB_WIDTH),
                         nsa_out_g[l][nsa_cols].reshape(1, NSA_WIDTH), w_o,
                         ln2_g[l].reshape(1, D), ffn_w_in[l].astype(MXU_DTYPE), conv_wb,
                         ffn_w_down[l].astype(MXU_DTYPE), final_g.reshape(1, D), tm_ffn, fc, l == L - 1)
    return x
```

```python
import functools
import math

import jax
import jax.numpy as jnp
from jax import lax
from jax.experimental import pallas as pl
from jax.experimental.pallas import tpu as pltpu

D_MODEL = 1024
HEAD_DIM = 64
SB_HEADS = 8
NSA_HEADS = 8
NSA_KV_HEADS = 2
NSA_GROUP = NSA_HEADS // NSA_KV_HEADS
SB_WIDTH = SB_HEADS * HEAD_DIM
NSA_WIDTH = NSA_HEADS * HEAD_DIM
KV_WIDTH = NSA_KV_HEADS * HEAD_DIM
N_BRANCH = 3
CMP_LEN = 32
CMP_STRIDE = 16
CMP_HIDDEN = 256
SEL_LEN = 64
SEL_TOPK = 16
N_LOCAL = 2
WINDOW = 512
ROPE_THETA = 10000.0
D_FF = 2816
CONV_W = 3
EPS = 1e-6
NEG = -1e30
FORCED_SCORE = 1e6

F32 = jnp.float32
MXU_DTYPE = jnp.bfloat16
LANES = 128
SUBLANES = 8
VMEM_LIMIT = 56 * 1024 * 1024
QK_SCALE = HEAD_DIM ** -0.5
EXP_ZERO_BELOW = -104.0

_NT = (((1,), (1,)), ((), ()))


def _cparams(sem):
    return pltpu.CompilerParams(dimension_semantics=sem, vmem_limit_bytes=VMEM_LIMIT)


def _dot(a, b):
    return jnp.dot(a, b, preferred_element_type=F32)


def _dot_nt(a, b):
    return lax.dot_general(a, b, _NT, preferred_element_type=F32)


def _split_hi_lo(a):
    hi = a.astype(MXU_DTYPE)
    lo = (a - hi.astype(F32)).astype(MXU_DTYPE)
    return hi, lo


def _iota(shape, dim):
    return lax.broadcasted_iota(jnp.int32, shape, dim)


def _mod_kernel(c_ref, w_ref, b_ref, o_ref):
    c = c_ref[...]
    a = c * jax.nn.sigmoid(c)
    o_ref[...] = jnp.dot(a, w_ref[...], preferred_element_type=F32,
                         precision=lax.Precision.HIGHEST) + b_ref[...]


def _modulation(c, w_ada, b_ada):
    L, D, D6 = w_ada.shape
    B = c.shape[0]
    n = D6 // D
    return pl.pallas_call(
        _mod_kernel,
        out_shape=jax.ShapeDtypeStruct((L, B, D6), F32),
        grid=(L, n),
        in_specs=[pl.BlockSpec((B, D), lambda l, j: (0, 0)),
                  pl.BlockSpec((None, D, D), lambda l, j: (l, 0, j)),
                  pl.BlockSpec((None, 1, D), lambda l, j: (l, 0, j))],
        out_specs=pl.BlockSpec((None, B, D), lambda l, j: (l, 0, j)),
        compiler_params=_cparams(("parallel", "parallel")),
        name="adaln_mod",
    )(c, w_ada, b_ada.reshape(L, 1, D6))


def _rope_table_kernel(inv_ref, cos_ref, sin_ref, *, rows, stride, offset):
    i = pl.program_id(0)
    r = _iota((rows, LANES), 0) + i * rows
    pos = (r * stride + offset).astype(F32)
    ang = pos * inv_ref[...]
    lane = _iota((rows, LANES), 1)
    first = (lane % HEAD_DIM) < (HEAD_DIM // 2)
    s = jnp.sin(ang)
    cos_ref[...] = jnp.cos(ang)
    sin_ref[...] = jnp.where(first, -s, s)


def _rope_tables(inv_lanes, n_rows, stride, offset):
    rows = min(n_rows, 512)
    kern = functools.partial(_rope_table_kernel, rows=rows, stride=stride, offset=offset)
    return pl.pallas_call(
        kern,
        out_shape=(jax.ShapeDtypeStruct((n_rows, LANES), F32),) * 2,
        grid=(n_rows // rows,),
        in_specs=[pl.BlockSpec((1, LANES), lambda i: (0, 0))],
        out_specs=(pl.BlockSpec((rows, LANES), lambda i: (i, 0)),) * 2,
        compiler_params=_cparams(("parallel",)),
        name="rope_tables",
    )(inv_lanes)


def _apply_rope(v, cos, sin_signed):
    lane = _iota(v.shape, 1)
    first = (lane % HEAD_DIM) < (HEAD_DIM // 2)
    half = HEAD_DIM // 2
    partner = jnp.where(first, pltpu.roll(v, LANES - half, 1), pltpu.roll(v, half, 1))
    return v * cos + partner * sin_signed


_PROJ_COLS = 3 * SB_WIDTH + NSA_WIDTH + 2 * N_BRANCH * KV_WIDTH + LANES


def _proj_kernel(x_ref, mod_ref, g_ref, w_ref, cos_ref, sin_ref,
                 sbq_ref, sbk_ref, sbv_ref, nq_ref, kc_ref, vc_ref,
                 ks_ref, vs_ref, kw_ref, vw_ref, gate_ref):
    x = x_ref[...]
    shift = mod_ref[0:1, :]
    scale = mod_ref[1:2, :]
    ms = jnp.mean(x * x, axis=-1, keepdims=True)
    h = (x * lax.rsqrt(ms + EPS)) * g_ref[...]
    h = h * (1.0 + scale) + shift
    hb = h.astype(MXU_DTYPE)
    cos = cos_ref[...]
    sin = sin_ref[...]

    def mm(lo, width):
        return _dot(hb, w_ref[:, lo:lo + width])

    o = 0
    sbq_ref[...] = (mm(o, SB_WIDTH) * QK_SCALE).astype(sbq_ref.dtype); o += SB_WIDTH
    sbk_ref[...] = mm(o, SB_WIDTH).astype(sbk_ref.dtype); o += SB_WIDTH
    sbv_ref[...] = mm(o, SB_WIDTH).astype(sbv_ref.dtype); o += SB_WIDTH
    nq = mm(o, NSA_WIDTH); o += NSA_WIDTH
    for p in range(NSA_WIDTH // LANES):
        blk = _apply_rope(nq[:, p * LANES:(p + 1) * LANES], cos, sin) * QK_SCALE
        nq_ref[:, p * LANES:(p + 1) * LANES] = blk.astype(nq_ref.dtype)
    rest = mm(o, 2 * N_BRANCH * KV_WIDTH + LANES)
    kc_ref[...] = rest[:, 0 * LANES:1 * LANES].astype(kc_ref.dtype)
    vc_ref[...] = rest[:, 1 * LANES:2 * LANES].astype(vc_ref.dtype)
    ks_ref[...] = _apply_rope(rest[:, 2 * LANES:3 * LANES], cos, sin).astype(ks_ref.dtype)
    kw_ref[...] = _apply_rope(rest[:, 4 * LANES:5 * LANES], cos, sin).astype(kw_ref.dtype)
    vs_ref[...] = rest[:, 3 * LANES:4 * LANES].T.astype(vs_ref.dtype)
    vw_ref[...] = rest[:, 5 * LANES:6 * LANES].T.astype(vw_ref.dtype)
    gate_ref[...] = jax.nn.sigmoid(rest[:, 6 * LANES:7 * LANES]).T


def _projection(x, mod_l, ln_g, w_cat, cos_t, sin_t, tm):
    B, T, D = x.shape
    row = lambda w: pl.BlockSpec((None, tm, w), lambda b, t: (b, t, 0))
    col = pl.BlockSpec((None, LANES, tm), lambda b, t: (b, 0, t))
    shp = lambda w, dt: jax.ShapeDtypeStruct((B, T, w), dt)
    shp_t = lambda dt: jax.ShapeDtypeStruct((B, LANES, T), dt)
    md = MXU_DTYPE
    return pl.pallas_call(
        _proj_kernel,
        out_shape=(shp(SB_WIDTH, md), shp(SB_WIDTH, md), shp(SB_WIDTH, md), shp(NSA_WIDTH, md),
                   shp(LANES, F32), shp(LANES, F32), shp(LANES, md), shp_t(md),
                   shp(LANES, md), shp_t(md), shp_t(F32)),
        grid=(B, T // tm),
        in_specs=[row(D),
                  pl.BlockSpec((None, 6, D), lambda b, t: (b, 0, 0)),
                  pl.BlockSpec((1, D), lambda b, t: (0, 0)),
                  pl.BlockSpec((D, _PROJ_COLS), lambda b, t: (0, 0)),
                  pl.BlockSpec((tm, LANES), lambda b, t: (t, 0)),
                  pl.BlockSpec((tm, LANES), lambda b, t: (t, 0))],
        out_specs=(row(SB_WIDTH), row(SB_WIDTH), row(SB_WIDTH), row(NSA_WIDTH),
                   row(LANES), row(LANES), row(LANES), col, row(LANES), col, col),
        compiler_params=_cparams(("parallel", "parallel")),
        name="norm_proj",
    )(x, mod_l, ln_g, w_cat, cos_t, sin_t)


def _sb_kernel(q_ref, k_ref, v_ref, uo_ref, o_ref, carry_ref, *, tq):
    n_pairs = q_ref.shape[1] // LANES
    low = _iota((tq, LANES), 1) < HEAD_DIM
    causal = _iota((tq, tq), 1) < _iota((tq, tq), 0)

    def sweep(tiles, rows):
        pairs = [slice(p * LANES, (p + 1) * LANES) for p in range(n_pairs)]
        heads = range(2 * n_pairs)
        resume = tiles[0][1] == "back"
        carry = [carry_ref[h] if resume else None for h in heads]
        out = [o_ref[rows, sl] if resume else None for sl in pairs]
        exists = [kb >= 0 for kb, _ in tiles]
        uo = uo_ref[...]
        starts = [pl.multiple_of(jnp.maximum(kb, 0) * tq, tq) for kb, _ in tiles]
        zs = []
        for k0 in starts:
            for sl in pairs:
                q = q_ref[rows, sl]
                kt = k_ref[pl.ds(k0, tq), sl]
                zq = jnp.zeros_like(q)
                zs.append(_dot_nt(jnp.where(low, q, zq), kt))
                zs.append(_dot_nt(jnp.where(low, zq, q), kt))
        lsns, sums = [], []
        for n, z in enumerate(zs):
            t = n // len(heads)
            mode = tiles[t][1]
            lsn = -(jnp.maximum(z, 0.0) + jnp.log(1.0 + jnp.exp(-jnp.abs(z))))
            if mode == "diag":
                lsn = jnp.where(causal, lsn, 0.0)
            elif mode == "near":
                lsn = jnp.where(exists[t], lsn, 0.0)
            hi, lo = _split_hi_lo(lsn)
            lsns.append(lsn)
            sums.append(_dot(hi, uo) + _dot(lo, uo))
        ws = []
        for n in range(len(zs)):
            t = n // len(heads)
            mode = tiles[t][1]
            h = n % len(heads)
            tail = sums[n][:, :tq]
            total = sums[n][:, tq:]
            if carry[h] is not None:
                tail = tail + carry[h]
                total = total + carry[h]
            w = jnp.exp(zs[n] + lsns[n] + tail)
            if mode == "diag":
                w = jnp.where(causal, w, 0.0)
            elif mode == "near":
                w = jnp.where(exists[t], w, 0.0)
            ws.append(w.astype(MXU_DTYPE))
            carry[h] = total
        for t, k0 in enumerate(starts):
            for p, sl in enumerate(pairs):
                vt = v_ref[pl.ds(k0, tq), sl]
                zv = jnp.zeros_like(vt)
                v2 = jnp.concatenate([jnp.where(low, vt, zv), jnp.where(low, zv, vt)], axis=0)
                n = t * len(heads) + 2 * p
                upd = _dot(jnp.concatenate(ws[n:n + 2], axis=1), v2)
                out[p] = upd if out[p] is None else out[p] + upd
        worst = carry[0]
        for h in heads:
            carry_ref[h] = carry[h]
            worst = jnp.maximum(worst, carry[h])
        for p, sl in enumerate(pairs):
            o_ref[rows, sl] = out[p]
        return (jnp.max(worst) > EXP_ZERO_BELOW).astype(jnp.int32)

    def cond(st):
        kb, live = st
        return jnp.logical_and(kb >= 0, live > 0)

    for sub in range(q_ref.shape[0] // tq):
        i = pl.program_id(1) * (q_ref.shape[0] // tq) + sub
        rows = slice(sub * tq, (sub + 1) * tq)

        def body(st):
            kb, _ = st
            return kb - 1, sweep([(kb, "back")], rows)

        first = sweep([(i, "diag"), (i - 1, "near"), (i - 2, "near")], rows)
        lax.while_loop(cond, body, (i - 3, first))


def _sb_attention(sbq, sbk, sbv, tq, rows_per_step):
    B, T, W = sbq.shape
    r = jnp.arange(tq)
    u = (r[:, None] > r[None, :]).astype(MXU_DTYPE)
    uo = jnp.concatenate([u, jnp.ones((tq, tq), MXU_DTYPE)], axis=1)
    kern = functools.partial(_sb_kernel, tq=tq)
    return pl.pallas_call(
        kern,
        out_shape=jax.ShapeDtypeStruct((B, T, W), F32),
        grid=(B, T // rows_per_step),
        in_specs=[pl.BlockSpec((None, rows_per_step, W), lambda b, i: (b, i, 0)),
                  pl.BlockSpec((None, T, W), lambda b, i: (b, 0, 0)),
                  pl.BlockSpec((None, T, W), lambda b, i: (b, 0, 0)),
                  pl.BlockSpec((tq, 2 * tq), lambda b, i: (0, 0))],
        out_specs=pl.BlockSpec((None, rows_per_step, W), lambda b, i: (b, i, 0)),
        scratch_shapes=[pltpu.VMEM((W // HEAD_DIM, tq, tq), F32)],
        compiler_params=_cparams(("parallel", "parallel")),
        name="sb_attention",
    )(sbq, sbk, sbv, uo)


def _gelu_tanh(x):
    c = math.sqrt(2.0 / math.pi)
    return 0.5 * x * (1.0 + jnp.tanh(c * (x + 0.044715 * (x * x * x))))


def _compress_kernel(kc_ref, vc_ref, wck_ref, w1k_ref, w2k_ref, pk_ref, wcv_ref, w1v_ref, w2v_ref, pv_ref,
                     cos_ref, sin_ref, ko_ref, vo_ref, *, nch):
    rowi = _iota((nch, LANES), 0)

    def one(c_ref, wc_ref, w1_ref, w2_ref, p_ref, rope):
        bias = _dot(p_ref[...], w1_ref[...])[0:1, :]
        proj = jnp.zeros((nch, 2 * NSA_KV_HEADS * CMP_HIDDEN), F32)
        for l in range(CMP_STRIDE):
            x_l = c_ref[pl.ds(l, nch, stride=CMP_STRIDE), :].astype(MXU_DTYPE)
            proj = proj + _dot(x_l, wc_ref[l])
        acc = jnp.zeros((nch, LANES), F32)
        for g in range(NSA_KV_HEADS):
            a = proj[:, (2 * g) * CMP_HIDDEN:(2 * g + 1) * CMP_HIDDEN]
            b = proj[:, (2 * g + 1) * CMP_HIDDEN:(2 * g + 2) * CMP_HIDDEN]
            hid = a + pltpu.roll(b, nch - 1, 0) + bias
            acc = acc + _dot(_gelu_tanh(hid).astype(MXU_DTYPE), w2_ref[g])
        if rope:
            acc = _apply_rope(acc, cos_ref[...], sin_ref[...])
        return jnp.where(rowi < nch - 1, acc, 0.0)

    ko_ref[...] = one(kc_ref, wck_ref, w1k_ref, w2k_ref, pk_ref, True).astype(ko_ref.dtype)
    vo_ref[...] = one(vc_ref, wcv_ref, w1v_ref, w2v_ref, pv_ref, False).T.astype(vo_ref.dtype)


def _pad_w2(w2):
    z = jnp.zeros_like(w2)
    return jnp.stack([jnp.concatenate([w2, z], 1), jnp.concatenate([z, w2], 1)]).astype(MXU_DTYPE)


def _chunk_weights(w1):
    half = CMP_STRIDE * HEAD_DIM
    top = w1[:half].reshape(CMP_STRIDE, HEAD_DIM, CMP_HIDDEN)
    bot = w1[half:].reshape(CMP_STRIDE, HEAD_DIM, CMP_HIDDEN)
    tb = jnp.concatenate([top, bot], axis=2)
    z = jnp.zeros_like(tb)
    return jnp.concatenate([jnp.concatenate([tb, z], axis=2),
                            jnp.concatenate([z, tb], axis=2)], axis=1).astype(MXU_DTYPE)


def _compress(kc, vc, w1k, w2k, pk, w1v, w2v, pv, cos_c, sin_c):
    B, T, _ = kc.shape
    nch = T // CMP_STRIDE
    flat = CMP_STRIDE * HEAD_DIM
    posf = lambda p: jnp.broadcast_to(p.reshape(1, CMP_LEN * HEAD_DIM), (SUBLANES, CMP_LEN * HEAD_DIM)).astype(MXU_DTYPE)
    full = lambda *s: pl.BlockSpec(s, lambda b: (0,) * len(s))
    wide = 2 * NSA_KV_HEADS * CMP_HIDDEN
    kern = functools.partial(_compress_kernel, nch=nch)
    return pl.pallas_call(
        kern,
        out_shape=(jax.ShapeDtypeStruct((B, nch, LANES), MXU_DTYPE),
                   jax.ShapeDtypeStruct((B, LANES, nch), MXU_DTYPE)),
        grid=(B,),
        in_specs=[pl.BlockSpec((None, T, LANES), lambda b: (b, 0, 0)),
                  pl.BlockSpec((None, T, LANES), lambda b: (b, 0, 0)),
                  full(CMP_STRIDE, LANES, wide), full(2 * flat, CMP_HIDDEN), full(2, CMP_HIDDEN, LANES),
                  full(SUBLANES, 2 * flat),
                  full(CMP_STRIDE, LANES, wide), full(2 * flat, CMP_HIDDEN), full(2, CMP_HIDDEN, LANES),
                  full(SUBLANES, 2 * flat),
                  full(nch, LANES), full(nch, LANES)],
        out_specs=(pl.BlockSpec((None, nch, LANES), lambda b: (b, 0, 0)),
                   pl.BlockSpec((None, LANES, nch), lambda b: (b, 0, 0))),
        compiler_params=_cparams(("parallel",)),
        name="nsa_compress",
    )(kc, vc, _chunk_weights(w1k), w1k.astype(MXU_DTYPE), _pad_w2(w2k), posf(pk),
      _chunk_weights(w1v), w1v.astype(MXU_DTYPE), _pad_w2(w2v), posf(pv), cos_c, sin_c)


def _nsa_kernel(q_ref, gt_ref, kcmp_ref, vcmpt_ref, ks_ref, vst_ref, kw_ref, vwt_ref,
                ovt_ref, eselt_ref, o_ref, qa_ref, s_ref, co_ref, *, tq, wide, n_sel, nsp):
    i = pl.program_id(1)
    q0 = i * tq
    ncmp = kcmp_ref.shape[0]
    top = min(SEL_TOPK, n_sel)
    heads = range(NSA_HEADS)
    lane = _iota((tq, LANES), 1)

    for g in range(NSA_KV_HEADS):
        in_group = (lane // HEAD_DIM) == g
        for p in range(NSA_GROUP):
            blk = q_ref[:, p * LANES:(p + 1) * LANES]
            qa_ref[g, p * tq:(p + 1) * tq, :LANES] = jnp.where(in_group, blk, jnp.zeros_like(blk))

    def fresh_state():
        return [(jnp.full((1, tq), NEG, F32), jnp.zeros((HEAD_DIM + SUBLANES, tq), F32)) for _ in heads]

    def per_head(s_ts):
        return [s_ts[g][:, hl * tq:(hl + 1) * tq] for g in range(NSA_KV_HEADS) for hl in range(NSA_GROUP)]

    def flash(state, xs, bias, vt_ref, k0, w):
        new = []
        for g in range(NSA_KV_HEADS):
            vt = jnp.concatenate([vt_ref[g * HEAD_DIM:(g + 1) * HEAD_DIM, pl.ds(k0, w)],
                                  jnp.ones((SUBLANES, w), MXU_DTYPE)], axis=0)
            for hl in range(NSA_GROUP):
                m_old, acc_old = state[g * NSA_GROUP + hl]
                x = xs[g * NSA_GROUP + hl]
                if bias is not None:
                    x = x + bias
                m_new = jnp.maximum(m_old, jnp.max(x, axis=0, keepdims=True))
                p = jnp.exp(x - m_new).astype(MXU_DTYPE)
                new.append((m_new, jnp.exp(m_old - m_new) * acc_old + _dot(vt, p)))
        return new

    def gated(state, branch):
        res = []
        for h in heads:
            acc = state[h][1]
            r = N_BRANCH * h + branch
            scale = gt_ref[r:r + 1, :] * (1.0 / jnp.maximum(acc[HEAD_DIM:HEAD_DIM + 1, :], 1e-6))
            res.append(acc[:HEAD_DIM, :] * scale)
        return res

    n_win = WINDOW // tq

    def win_scores(j):
        k0 = pl.multiple_of(jnp.maximum(q0 - (n_win - j) * tq, 0), tq)
        kt = kw_ref[pl.ds(k0, tq), :]
        return [_dot_nt(kt, qa_ref[g, :, :LANES]) for g in range(NSA_KV_HEADS)], k0

    def window_branch():
        krow = _iota((tq, tq), 0)
        qcol = _iota((tq, tq), 1)
        causal_bias = jnp.where(krow <= qcol, 0.0, NEG)
        oldest_bias = jnp.where(krow > qcol, 0.0, NEG)
        w_scores, w_k0 = win_scores(0)
        win_state = fresh_state()
        for j in range(n_win + 1):
            nxt = win_scores(j + 1) if j < n_win else None
            bias = oldest_bias if j == 0 else (causal_bias if j == n_win else None)
            new_state = flash(win_state, per_head(w_scores), bias, vwt_ref, w_k0, tq)
            if j < n_win:
                live = i >= n_win - j
                new_state = [(jnp.where(live, mn, mo), jnp.where(live, an, ao))
                             for (mn, an), (mo, ao) in zip(new_state, win_state)]
                w_scores, w_k0 = nxt
            win_state = new_state
        return gated(win_state, 2)

    def compress_select(n_blk, n_row):
        kcmp = kcmp_ref[:n_row, :]
        valid_c = (_iota((n_row, tq), 0) * CMP_STRIDE + (CMP_LEN - 1)) <= (q0 + _iota((n_row, tq), 1))
        bias_c = jnp.where(valid_c, 0.0, NEG)
        keep_c = jnp.where(valid_c, 1.0, 0.0)
        n_chunks = n_blk // SUBLANES
        jrow = _iota((SUBLANES, tq), 0)
        qblk = (q0 + _iota((SUBLANES, tq), 1)) // SEL_LEN
        cmp_out = [None] * NSA_HEADS
        imp_ts = []
        for g in range(NSA_KV_HEADS):
            s_t = _dot_nt(kcmp, qa_ref[g, :, :LANES])
            vct = vcmpt_ref[g * HEAD_DIM:(g + 1) * HEAD_DIM, :n_row]
            psum = None
            for hl in range(NSA_GROUP):
                h = g * NSA_GROUP + hl
                x = s_t[:, hl * tq:(hl + 1) * tq] + bias_c
                p = jnp.exp(x - jnp.max(x, axis=0, keepdims=True)) * keep_c
                p = p * (1.0 / jnp.maximum(jnp.sum(p, axis=0, keepdims=True), 1e-6))
                psum = p if psum is None else psum + p
                cmp_out[h] = gt_ref[N_BRANCH * h:N_BRANCH * h + 1, :] * _dot(vct, p.astype(MXU_DTYPE))
            ph, plo = _split_hi_lo(psum)
            ovt = ovt_ref[:n_blk, :n_row]
            imp_ts.append(_dot(ovt, ph) + _dot(ovt, plo))
        for h, res in enumerate(window_branch()):
            co_ref[h] = cmp_out[h] + res
        for g in range(NSA_KV_HEADS):
            imp_t = imp_ts[g]
            scores = []
            for c in range(n_chunks):
                dist = qblk - (jrow + c * SUBLANES)
                forced = (dist >= 0) & (dist < N_LOCAL)
                if c == 0:
                    forced = forced | (jrow == 0)
                imp_c = imp_t[c * SUBLANES:(c + 1) * SUBLANES, :]
                scores.append(jnp.where(dist < 0, -1.0, jnp.where(forced, FORCED_SCORE, imp_c)))
            ranks = [jnp.zeros((SUBLANES, tq), F32) for _ in range(n_chunks)]
            for i2 in range(min(n_blk, n_sel)):
                c2, r2 = divmod(i2, SUBLANES)
                r = jnp.broadcast_to(scores[c2][r2:r2 + 1, :], (SUBLANES, tq))
                for c in range(n_chunks):
                    if c < c2:
                        beats = r > scores[c]
                    elif c > c2:
                        beats = r >= scores[c]
                    else:
                        beats = (r > scores[c]) | ((r == scores[c]) & (jrow > r2))
                    ranks[c] = ranks[c] + jnp.where(beats, 1.0, 0.0)
            nsel = [jnp.where((ranks[c] < top) & (scores[c] >= 0.0), 0.0, NEG) for c in range(n_chunks)]
            if n_blk < nsp:
                nsel.append(jnp.full((nsp - n_blk, tq), NEG, F32))
            if nsp < LANES:
                nsel.append(jnp.zeros((LANES - nsp, tq), F32))
            nsel_q = jnp.concatenate(nsel, axis=0).T.astype(MXU_DTYPE)
            for p in range(NSA_GROUP):
                qa_ref[g, p * tq:(p + 1) * tq, LANES:] = nsel_q

    cls_blk = 2 * SUBLANES
    n_cls = -(-nsp // cls_blk)
    cls = jnp.minimum(((q0 + tq - 1) // SEL_LEN) // cls_blk, n_cls - 1)
    for k in range(n_cls):
        n_blk = min(cls_blk * (k + 1), nsp)
        n_row = min(-(-n_blk * (SEL_LEN // CMP_STRIDE) // LANES) * LANES, ncmp)

        @pl.when(cls == k)
        def _():
            compress_select(n_blk, n_row)

    t_max = eselt_ref.shape[0] // wide - 1

    def tile_start(t):
        return pl.multiple_of(jnp.minimum(t, t_max) * wide, wide)

    def sel_scores(t, slot):
        k0 = tile_start(t)
        kaug = jnp.concatenate([ks_ref[pl.ds(k0, wide), :], eselt_ref[pl.ds(k0, wide), :]], axis=1)
        s_ts = [_dot_nt(kaug, qa_ref[g]) for g in range(NSA_KV_HEADS)]
        for h, x in enumerate(per_head(s_ts)):
            s_ref[slot, h] = x

    def sel_flash(state, t, slot, causal):
        bias = None
        if causal:
            bias = jnp.where(_iota((wide, tq), 0) - _iota((wide, tq), 1) <= q0 - t * wide, 0.0, NEG)
        xs = [s_ref[slot, h] for h in heads]
        return flash(state, xs, bias, vst_ref, tile_start(t), wide)

    def sel_body(u, state):
        sel_scores(2 * u + 1, 1)
        state = sel_flash(state, 2 * u, 0, False)
        sel_scores(2 * u + 2, 0)
        return sel_flash(state, 2 * u + 1, 1, False)

    n_tiles = (q0 + tq - 1) // wide + 1
    n_even = n_tiles + n_tiles % 2
    sel_scores(0, 0)
    sel_state = lax.fori_loop(0, n_even // 2 - 1, sel_body, fresh_state())
    sel_scores(n_even - 1, 1)
    sel_state = sel_flash(sel_state, n_even - 2, 0, True)

    sel_state = sel_flash(sel_state, n_even - 1, 1, True)
    outs = [co_ref[h] + res for h, res in enumerate(gated(sel_state, 1))]

    for hl in range(NSA_GROUP):
        pair_t = jnp.concatenate([outs[hl], outs[NSA_GROUP + hl]], axis=0)
        o_ref[:, hl * LANES:(hl + 1) * LANES] = pair_t.T


def _nsa_attention(nq, gates_t, kcmp, vcmp_t, ks, vs_t, kw, vw_t, tq, wide):
    B, T, _ = nq.shape
    ncmp = kcmp.shape[1]
    n_sel = T // SEL_LEN
    nsp = -(-n_sel // SUBLANES) * SUBLANES
    cs = jnp.arange(ncmp) * CMP_STRIDE
    ss = jnp.arange(nsp) * SEL_LEN
    ov = jnp.clip(jnp.minimum(cs[None, :] + CMP_LEN, ss[:, None] + SEL_LEN)
                  - jnp.maximum(cs[None, :], ss[:, None]), 0, None).astype(F32) / CMP_LEN
    ov = jnp.where((jnp.arange(nsp) < n_sel)[:, None] & (jnp.arange(ncmp) < ncmp - 1)[None, :], ov, 0.0)
    ovt = ov.astype(MXU_DTYPE)
    esel_t = ((jnp.arange(T) // SEL_LEN)[:, None] == jnp.arange(LANES)[None, :]).astype(MXU_DTYPE)
    kern = functools.partial(_nsa_kernel, tq=tq, wide=wide, n_sel=n_sel, nsp=nsp)
    per_b = lambda r, w: pl.BlockSpec((None, r, w), lambda b, i: (b, 0, 0))
    return pl.pallas_call(
        kern,
        out_shape=jax.ShapeDtypeStruct((B, T, NSA_WIDTH), F32),
        grid=(B, T // tq),
        in_specs=[pl.BlockSpec((None, tq, NSA_WIDTH), lambda b, i: (b, i, 0)),
                  pl.BlockSpec((None, LANES, tq), lambda b, i: (b, 0, i)),
                  per_b(ncmp, LANES), per_b(LANES, ncmp),
                  per_b(T, LANES), per_b(LANES, T), per_b(T, LANES), per_b(LANES, T),
                  pl.BlockSpec((nsp, ncmp), lambda b, i: (0, 0)),
                  pl.BlockSpec((T, LANES), lambda b, i: (0, 0))],
        out_specs=pl.BlockSpec((None, tq, NSA_WIDTH), lambda b, i: (b, i, 0)),
        scratch_shapes=[pltpu.VMEM((NSA_KV_HEADS, NSA_GROUP * tq, 2 * LANES), MXU_DTYPE),
                        pltpu.VMEM((2, NSA_HEADS, wide, tq), F32),
                        pltpu.VMEM((NSA_HEADS, HEAD_DIM, tq), F32)],
        compiler_params=_cparams(("parallel", "parallel")),
        name="nsa_attention",
    )(nq, gates_t, kcmp, vcmp_t, ks, vs_t, kw, vw_t, ovt, esel_t)


def _rms(v, g):
    return (v * lax.rsqrt(jnp.mean(v * v, axis=-1, keepdims=True) + EPS)) * g


def _ffn_kernel(x_ref, osb_ref, onsa_ref, mod_ref, gsb_ref, gnsa_ref, wo_ref, g_ref, win_ref, cw_ref,
                wdn_ref, fg_ref, o_ref, prev_ref, ubuf_ref, z_ref, *, tm, fc, final_norm):
    t = pl.program_id(1)

    @pl.when(t == 0)
    def _():
        prev_ref[...] = jnp.zeros_like(prev_ref)

    a_sb = _rms(osb_ref[...], gsb_ref[...]).astype(MXU_DTYPE)
    a_nsa = _rms(onsa_ref[...], gnsa_ref[...]).astype(MXU_DTYPE)
    attn = _dot(a_sb, wo_ref[:SB_WIDTH, :]) + _dot(a_nsa, wo_ref[SB_WIDTH:, :])
    x = x_ref[...] + mod_ref[2:3, :] * attn
    h = _rms(x, g_ref[...]) * (1.0 + mod_ref[4:5, :]) + mod_ref[3:4, :]
    hb = h.astype(MXU_DTYPE)
    def conv(u, c0, slot):
        ubuf_ref[slot, :SUBLANES, :] = prev_ref[:, c0:c0 + fc]
        ubuf_ref[slot, SUBLANES:, :] = u
        prev_ref[:, c0:c0 + fc] = u[tm - SUBLANES:, :]
        u1 = ubuf_ref[slot, SUBLANES - 1:SUBLANES - 1 + tm, :]
        u2 = ubuf_ref[slot, SUBLANES - 2:SUBLANES - 2 + tm, :]
        cw = cw_ref[:, c0:c0 + fc]
        return cw[2:3, :] * u + cw[1:2, :] * u1 + cw[0:1, :] * u2 + cw[3:4, :]

    def up(c):
        return (_dot(hb, win_ref[:, c * fc:(c + 1) * fc]),
                _dot(hb, win_ref[:, D_FF + c * fc:D_FF + (c + 1) * fc]))

    n_chunks = D_FF // fc
    ua, ub = up(0)
    for c in range(n_chunks):
        nxt = up(c + 1) if c + 1 < n_chunks else None
        ya = conv(ua, c * fc, 2 * (c % 2))
        yb = conv(ub, D_FF + c * fc, 2 * (c % 2) + 1)
        z_ref[:, c * fc:(c + 1) * fc] = ((ya * jax.nn.sigmoid(ya)) * yb).astype(MXU_DTYPE)
        if nxt is not None:
            ua, ub = nxt
    y = x + mod_ref[5:6, :] * _dot(z_ref[...], wdn_ref[...])
    if final_norm:
        y = _rms(y, fg_ref[...])
    o_ref[...] = y


def _outproj_ffn(x, o_sb, o_nsa, mod_l, g_sb, g_nsa, w_out, ln_g, w_in, conv_wb, w_down, final_g,
                 tm, fc, final_norm):
    B, T, D = x.shape
    row = lambda w: pl.BlockSpec((None, tm, w), lambda b, t: (b, t, 0))
    vec = lambda w: pl.BlockSpec((1, w), lambda b, t: (0, 0))
    const = lambda r, c: pl.BlockSpec((r, c), lambda b, t: (0, 0), pipeline_mode=pl.Buffered(1))
    kern = functools.partial(_ffn_kernel, tm=tm, fc=fc, final_norm=final_norm)
    return pl.pallas_call(
        kern,
        out_shape=jax.ShapeDtypeStruct((B, T, D), F32),
        grid=(B, T // tm),
        in_specs=[row(D), row(SB_WIDTH), row(NSA_WIDTH),
                  pl.BlockSpec((None, 6, D), lambda b, t: (b, 0, 0)),
                  vec(SB_WIDTH), vec(NSA_WIDTH), const(D, D), vec(D),
                  const(D, 2 * D_FF), const(SUBLANES, 2 * D_FF), const(D_FF, D), vec(D)],
        out_specs=row(D),
        scratch_shapes=[pltpu.VMEM((SUBLANES, 2 * D_FF), F32),
                        pltpu.VMEM((4, SUBLANES + tm, fc), F32),
                        pltpu.VMEM((tm, D_FF), MXU_DTYPE)],
        compiler_params=_cparams(("parallel", "arbitrary")),
        name="outproj_ffn",
    )(x, o_sb, o_nsa, mod_l, g_sb, g_nsa, w_out, ln_g, w_in, conv_wb, w_down, final_g)


_NSA_HEAD_ORDER = [h for p in range(NSA_GROUP) for h in (p, NSA_GROUP + p)]


def _head_cols(order):
    return jnp.asarray([h * HEAD_DIM + d for h in order for d in range(HEAD_DIM)], jnp.int32)


def _prep_w_in(w):
    nq0 = 3 * SB_WIDTH
    kv0 = nq0 + NSA_WIDTH
    gl0 = kv0 + 2 * N_BRANCH * KV_WIDTH
    nq = w[:, nq0:kv0][:, _head_cols(_NSA_HEAD_ORDER)]
    gl = jnp.pad(w[:, gl0:], ((0, 0), (0, LANES - (w.shape[1] - gl0))))
    return jnp.concatenate([w[:, :nq0], nq, w[:, kv0:gl0], gl], axis=1).astype(MXU_DTYPE)


def kernel(x, c, ln1_g, ln2_g, w_ada, b_ada, w_in, cmp_pos_k, cmp_w1_k, cmp_w2_k, cmp_pos_v, cmp_w1_v, cmp_w2_v, sb_out_g, nsa_out_g, w_out, ffn_w_in, ffn_conv_w, ffn_conv_b, ffn_w_down, final_g):
    B, T, D = x.shape
    L = w_in.shape[0]
    assert D == D_MODEL and T % 512 == 0 and T // SEL_LEN <= LANES
    tm_proj = 512
    tm_ffn = 256
    fc = 256
    tq = 128
    nch = T // CMP_STRIDE

    half = HEAD_DIM // 2
    inv = ROPE_THETA ** (-jnp.arange(half, dtype=F32) / half)
    inv_lanes = jnp.tile(inv, LANES // half).reshape(1, LANES)
    cos_t, sin_t = _rope_tables(inv_lanes, T, 1, 0)
    cos_c, sin_c = _rope_tables(inv_lanes, nch, CMP_STRIDE, CMP_LEN - 1)

    mod = _modulation(c, w_ada, b_ada).reshape(L, B, 6, D)
    nsa_cols = _head_cols(_NSA_HEAD_ORDER)

    for l in range(L):
        mod_l = mod[l]
        (sbq, sbk, sbv, nq, kc, vc, ks, vs, kw, vw, gates) = _projection(
            x, mod_l, ln1_g[l].reshape(1, D), _prep_w_in(w_in[l]), cos_t, sin_t, tm_proj)
        o_sb = _sb_attention(sbq, sbk, sbv, tq, 2 * tq)
        kcmp, vcmp = _compress(kc, vc, cmp_w1_k[l], cmp_w2_k[l], cmp_pos_k[l],
                               cmp_w1_v[l], cmp_w2_v[l], cmp_pos_v[l], cos_c, sin_c)
        o_nsa = _nsa_attention(nq, gates, kcmp, vcmp, ks, vs, kw, vw, 2 * tq, 2 * tq)
        w_o = jnp.concatenate([w_out[l][:SB_WIDTH], w_out[l][SB_WIDTH:][nsa_cols]], axis=0).astype(MXU_DTYPE)
        conv_wb = jnp.concatenate([ffn_conv_w[l], ffn_conv_b[l][None, :],
                                   jnp.zeros((SUBLANES - CONV_W - 1, 2 * D_FF), F32)], axis=0)
        x = _outproj_ffn(x, o_sb, o_nsa, mod_l, sb_out_g[l].reshape(1, SB_WIDTH),
                         nsa_out_g[l][nsa_cols].reshape(1, NSA_WIDTH), w_o,
                         ln2_g[l].reshape(1, D), ffn_w_in[l].astype(MXU_DTYPE), conv_wb,
                         ffn_w_down[l].astype(MXU_DTYPE), final_g.reshape(1, D), tm_ffn, fc, l == L - 1)
    return x
```

```python
import functools
import math

import jax
import jax.numpy as jnp
from jax import lax
from jax.experimental import pallas as pl
from jax.experimental.pallas import tpu as pltpu

D_MODEL = 1024
HEAD_DIM = 64
SB_HEADS = 8
NSA_HEADS = 8
NSA_KV_HEADS = 2
NSA_GROUP = NSA_HEADS // NSA_KV_HEADS
SB_WIDTH = SB_HEADS * HEAD_DIM
NSA_WIDTH = NSA_HEADS * HEAD_DIM
KV_WIDTH = NSA_KV_HEADS * HEAD_DIM
N_BRANCH = 3
CMP_LEN = 32
CMP_STRIDE = 16
CMP_HIDDEN = 256
SEL_LEN = 64
SEL_TOPK = 16
N_LOCAL = 2
WINDOW = 512
ROPE_THETA = 10000.0
D_FF = 2816
CONV_W = 3
EPS = 1e-6
NEG = -1e30
FORCED_SCORE = 1e6

F32 = jnp.float32
MXU_DTYPE = jnp.bfloat16
LANES = 128
SUBLANES = 8
VMEM_LIMIT = 56 * 1024 * 1024
QK_SCALE = HEAD_DIM ** -0.5
EXP_ZERO_BELOW = -104.0

_NT = (((1,), (1,)), ((), ()))


def _cparams(sem):
    return pltpu.CompilerParams(dimension_semantics=sem, vmem_limit_bytes=VMEM_LIMIT)


def _dot(a, b):
    return jnp.dot(a, b, preferred_element_type=F32)


def _dot_nt(a, b):
    return lax.dot_general(a, b, _NT, preferred_element_type=F32)


def _split_hi_lo(a):
    hi = a.astype(MXU_DTYPE)
    lo = (a - hi.astype(F32)).astype(MXU_DTYPE)
    return hi, lo


def _iota(shape, dim):
    return lax.broadcasted_iota(jnp.int32, shape, dim)


def _mod_kernel(c_ref, w_ref, b_ref, o_ref):
    c = c_ref[...]
    a = c * jax.nn.sigmoid(c)
    o_ref[...] = jnp.dot(a, w_ref[...], preferred_element_type=F32,
                         precision=lax.Precision.HIGHEST) + b_ref[...]


def _modulation(c, w_ada, b_ada):
    L, D, D6 = w_ada.shape
    B = c.shape[0]
    n = D6 // D
    return pl.pallas_call(
        _mod_kernel,
        out_shape=jax.ShapeDtypeStruct((L, B, D6), F32),
        grid=(L, n),
        in_specs=[pl.BlockSpec((B, D), lambda l, j: (0, 0)),
                  pl.BlockSpec((None, D, D), lambda l, j: (l, 0, j)),
                  pl.BlockSpec((None, 1, D), lambda l, j: (l, 0, j))],
        out_specs=pl.BlockSpec((None, B, D), lambda l, j: (l, 0, j)),
        compiler_params=_cparams(("parallel", "parallel")),
        name="adaln_mod",
    )(c, w_ada, b_ada.reshape(L, 1, D6))


def _rope_table_kernel(inv_ref, cos_ref, sin_ref, *, rows, stride, offset):
    i = pl.program_id(0)
    r = _iota((rows, LANES), 0) + i * rows
    pos = (r * stride + offset).astype(F32)
    ang = pos * inv_ref[...]
    lane = _iota((rows, LANES), 1)
    first = (lane % HEAD_DIM) < (HEAD_DIM // 2)
    s = jnp.sin(ang)
    cos_ref[...] = jnp.cos(ang)
    sin_ref[...] = jnp.where(first, -s, s)


def _rope_tables(inv_lanes, n_rows, stride, offset):
    rows = min(n_rows, 512)
    kern = functools.partial(_rope_table_kernel, rows=rows, stride=stride, offset=offset)
    return pl.pallas_call(
        kern,
        out_shape=(jax.ShapeDtypeStruct((n_rows, LANES), F32),) * 2,
        grid=(n_rows // rows,),
        in_specs=[pl.BlockSpec((1, LANES), lambda i: (0, 0))],
        out_specs=(pl.BlockSpec((rows, LANES), lambda i: (i, 0)),) * 2,
        compiler_params=_cparams(("parallel",)),
        name="rope_tables",
    )(inv_lanes)


def _apply_rope(v, cos, sin_signed):
    lane = _iota(v.shape, 1)
    first = (lane % HEAD_DIM) < (HEAD_DIM // 2)
    half = HEAD_DIM // 2
    partner = jnp.where(first, pltpu.roll(v, LANES - half, 1), pltpu.roll(v, half, 1))
    return v * cos + partner * sin_signed


_PROJ_COLS = 3 * SB_WIDTH + NSA_WIDTH + 2 * N_BRANCH * KV_WIDTH + LANES


def _proj_kernel(x_ref, mod_ref, g_ref, w_ref, cos_ref, sin_ref,
                 sbq_ref, sbk_ref, sbv_ref, nq_ref, kc_ref, vc_ref,
                 ks_ref, vs_ref, kw_ref, vw_ref, gate_ref):
    x = x_ref[...]
    shift = mod_ref[0:1, :]
    scale = mod_ref[1:2, :]
    ms = jnp.mean(x * x, axis=-1, keepdims=True)
    h = (x * lax.rsqrt(ms + EPS)) * g_ref[...]
    h = h * (1.0 + scale) + shift
    hb = h.astype(MXU_DTYPE)
    cos = cos_ref[...]
    sin = sin_ref[...]

    def mm(lo, width):
        return _dot(hb, w_ref[:, lo:lo + width])

    o = 0
    sbq_ref[...] = (mm(o, SB_WIDTH) * QK_SCALE).astype(sbq_ref.dtype); o += SB_WIDTH
    sbk_ref[...] = mm(o, SB_WIDTH).astype(sbk_ref.dtype); o += SB_WIDTH
    sbv_ref[...] = mm(o, SB_WIDTH).astype(sbv_ref.dtype); o += SB_WIDTH
    nq = mm(o, NSA_WIDTH); o += NSA_WIDTH
    for p in range(NSA_WIDTH // LANES):
        blk = _apply_rope(nq[:, p * LANES:(p + 1) * LANES], cos, sin) * QK_SCALE
        nq_ref[:, p * LANES:(p + 1) * LANES] = blk.astype(nq_ref.dtype)
    rest = mm(o, 2 * N_BRANCH * KV_WIDTH + LANES)
    kc_ref[...] = rest[:, 0 * LANES:1 * LANES].astype(kc_ref.dtype)
    vc_ref[...] = rest[:, 1 * LANES:2 * LANES].astype(vc_ref.dtype)
    ks_ref[...] = _apply_rope(rest[:, 2 * LANES:3 * LANES], cos, sin).astype(ks_ref.dtype)
    kw_ref[...] = _apply_rope(rest[:, 4 * LANES:5 * LANES], cos, sin).astype(kw_ref.dtype)
    vs_ref[...] = rest[:, 3 * LANES:4 * LANES].T.astype(vs_ref.dtype)
    vw_ref[...] = rest[:, 5 * LANES:6 * LANES].T.astype(vw_ref.dtype)
    gate_ref[...] = jax.nn.sigmoid(rest[:, 6 * LANES:7 * LANES]).T


def _projection(x, mod_l, ln_g, w_cat, cos_t, sin_t, tm):
    B, T, D = x.shape
    row = lambda w: pl.BlockSpec((None, tm, w), lambda b, t: (b, t, 0))
    col = pl.BlockSpec((None, LANES, tm), lambda b, t: (b, 0, t))
    shp = lambda w, dt: jax.ShapeDtypeStruct((B, T, w), dt)
    shp_t = lambda dt: jax.ShapeDtypeStruct((B, LANES, T), dt)
    md = MXU_DTYPE
    return pl.pallas_call(
        _proj_kernel,
        out_shape=(shp(SB_WIDTH, md), shp(SB_WIDTH, md), shp(SB_WIDTH, md), shp(NSA_WIDTH, md),
                   shp(LANES, F32), shp(LANES, F32), shp(LANES, md), shp_t(md),
                   shp(LANES, md), shp_t(md), shp_t(F32)),
        grid=(B, T // tm),
        in_specs=[row(D),
                  pl.BlockSpec((None, 6, D), lambda b, t: (b, 0, 0)),
                  pl.BlockSpec((1, D), lambda b, t: (0, 0)),
                  pl.BlockSpec((D, _PROJ_COLS), lambda b, t: (0, 0)),
                  pl.BlockSpec((tm, LANES), lambda b, t: (t, 0)),
                  pl.BlockSpec((tm, LANES), lambda b, t: (t, 0))],
        out_specs=(row(SB_WIDTH), row(SB_WIDTH), row(SB_WIDTH), row(NSA_WIDTH),
                   row(LANES), row(LANES), row(LANES), col, row(LANES), col, col),
        compiler_params=_cparams(("parallel", "parallel")),
        name="norm_proj",
    )(x, mod_l, ln_g, w_cat, cos_t, sin_t)


def _sb_kernel(q_ref, k_ref, v_ref, uo_ref, o_ref, carry_ref, *, tq):
    n_pairs = q_ref.shape[1] // LANES
    low = _iota((tq, LANES), 1) < HEAD_DIM
    causal = _iota((tq, tq), 1) < _iota((tq, tq), 0)

    def sweep(tiles, rows):
        pairs = [slice(p * LANES, (p + 1) * LANES) for p in range(n_pairs)]
        heads = range(2 * n_pairs)
        resume = tiles[0][1] == "back"
        carry = [carry_ref[h] if resume else None for h in heads]
        out = [o_ref[rows, sl] if resume else None for sl in pairs]
        exists = [kb >= 0 for kb, _ in tiles]
        uo = uo_ref[...]
        starts = [pl.multiple_of(jnp.maximum(kb, 0) * tq, tq) for kb, _ in tiles]
        zs = []
        for k0 in starts:
            for sl in pairs:
                q = q_ref[rows, sl]
                kt = k_ref[pl.ds(k0, tq), sl]
                zq = jnp.zeros_like(q)
                zs.append(_dot_nt(jnp.where(low, q, zq), kt))
                zs.append(_dot_nt(jnp.where(low, zq, q), kt))
        lsns, sums = [], []
        for n, z in enumerate(zs):
            t = n // len(heads)
            mode = tiles[t][1]
            lsn = -(jnp.maximum(z, 0.0) + jnp.log(1.0 + jnp.exp(-jnp.abs(z))))
            if mode == "diag":
                lsn = jnp.where(causal, lsn, 0.0)
            elif mode == "near":
                lsn = jnp.where(exists[t], lsn, 0.0)
            hi, lo = _split_hi_lo(lsn)
            lsns.append(lsn)
            sums.append(_dot(hi, uo) + _dot(lo, uo))
        ws = []
        for n in range(len(zs)):
            t = n // len(heads)
            mode = tiles[t][1]
            h = n % len(heads)
            tail = sums[n][:, :tq]
            total = sums[n][:, tq:]
            if carry[h] is not None:
                tail = tail + carry[h]
                total = total + carry[h]
            w = jnp.exp(zs[n] + lsns[n] + tail)
            if mode == "diag":
                w = jnp.where(causal, w, 0.0)
            elif mode == "near":
                w = jnp.where(exists[t], w, 0.0)
            ws.append(w.astype(MXU_DTYPE))
            carry[h] = total
        for t, k0 in enumerate(starts):
            for p, sl in enumerate(pairs):
                vt = v_ref[pl.ds(k0, tq), sl]
                zv = jnp.zeros_like(vt)
                v2 = jnp.concatenate([jnp.where(low, vt, zv), jnp.where(low, zv, vt)], axis=0)
                n = t * len(heads) + 2 * p
                upd = _dot(jnp.concatenate(ws[n:n + 2], axis=1), v2)
                out[p] = upd if out[p] is None else out[p] + upd
        worst = carry[0]
        for h in heads:
            carry_ref[h] = carry[h]
            worst = jnp.maximum(worst, carry[h])
        for p, sl in enumerate(pairs):
            o_ref[rows, sl] = out[p]
        return (jnp.max(worst) > EXP_ZERO_BELOW).astype(jnp.int32)

    def cond(st):
        kb, live = st
        return jnp.logical_and(kb >= 0, live > 0)

    for sub in range(q_ref.shape[0] // tq):
        i = pl.program_id(1) * (q_ref.shape[0] // tq) + sub
        rows = slice(sub * tq, (sub + 1) * tq)

        def body(st):
            kb, _ = st
            return kb - 1, sweep([(kb, "back")], rows)

        first = sweep([(i, "diag"), (i - 1, "near"), (i - 2, "near")], rows)
        lax.while_loop(cond, body, (i - 3, first))


def _sb_attention(sbq, sbk, sbv, tq, rows_per_step):
    B, T, W = sbq.shape
    r = jnp.arange(tq)
    u = (r[:, None] > r[None, :]).astype(MXU_DTYPE)
    uo = jnp.concatenate([u, jnp.ones((tq, tq), MXU_DTYPE)], axis=1)
    kern = functools.partial(_sb_kernel, tq=tq)
    return pl.pallas_call(
        kern,
        out_shape=jax.ShapeDtypeStruct((B, T, W), F32),
        grid=(B, T // rows_per_step),
        in_specs=[pl.BlockSpec((None, rows_per_step, W), lambda b, i: (b, i, 0)),
                  pl.BlockSpec((None, T, W), lambda b, i: (b, 0, 0)),
                  pl.BlockSpec((None, T, W), lambda b, i: (b, 0, 0)),
                  pl.BlockSpec((tq, 2 * tq), lambda b, i: (0, 0))],
        out_specs=pl.BlockSpec((None, rows_per_step, W), lambda b, i: (b, i, 0)),
        scratch_shapes=[pltpu.VMEM((W // HEAD_DIM, tq, tq), F32)],
        compiler_params=_cparams(("parallel", "parallel")),
        name="sb_attention",
    )(sbq, sbk, sbv, uo)


def _gelu_tanh(x):
    c = math.sqrt(2.0 / math.pi)
    return 0.5 * x * (1.0 + jnp.tanh(c * (x + 0.044715 * (x * x * x))))


def _compress_kernel(kc_ref, vc_ref, wck_ref, w1k_ref, w2k_ref, pk_ref, wcv_ref, w1v_ref, w2v_ref, pv_ref,
                     cos_ref, sin_ref, ko_ref, vo_ref, *, nch):
    rowi = _iota((nch, LANES), 0)

    def one(c_ref, wc_ref, w1_ref, w2_ref, p_ref, rope):
        bias = _dot(p_ref[...], w1_ref[...])[0:1, :]
        proj = jnp.zeros((nch, 2 * NSA_KV_HEADS * CMP_HIDDEN), F32)
        for l in range(CMP_STRIDE):
            x_l = c_ref[pl.ds(l, nch, stride=CMP_STRIDE), :].astype(MXU_DTYPE)
            proj = proj + _dot(x_l, wc_ref[l])
        acc = jnp.zeros((nch, LANES), F32)
        for g in range(NSA_KV_HEADS):
            a = proj[:, (2 * g) * CMP_HIDDEN:(2 * g + 1) * CMP_HIDDEN]
            b = proj[:, (2 * g + 1) * CMP_HIDDEN:(2 * g + 2) * CMP_HIDDEN]
            hid = a + pltpu.roll(b, nch - 1, 0) + bias
            acc = acc + _dot(_gelu_tanh(hid).astype(MXU_DTYPE), w2_ref[g])
        if rope:
            acc = _apply_rope(acc, cos_ref[...], sin_ref[...])
        return jnp.where(rowi < nch - 1, acc, 0.0)

    ko_ref[...] = one(kc_ref, wck_ref, w1k_ref, w2k_ref, pk_ref, True).astype(ko_ref.dtype)
    vo_ref[...] = one(vc_ref, wcv_ref, w1v_ref, w2v_ref, pv_ref, False).T.astype(vo_ref.dtype)


def _pad_w2(w2):
    z = jnp.zeros_like(w2)
    return jnp.stack([jnp.concatenate([w2, z], 1), jnp.concatenate([z, w2], 1)]).astype(MXU_DTYPE)


def _chunk_weights(w1):
    half = CMP_STRIDE * HEAD_DIM
    top = w1[:half].reshape(CMP_STRIDE, HEAD_DIM, CMP_HIDDEN)
    bot = w1[half:].reshape(CMP_STRIDE, HEAD_DIM, CMP_HIDDEN)
    tb = jnp.concatenate([top, bot], axis=2)
    z = jnp.zeros_like(tb)
    return jnp.concatenate([jnp.concatenate([tb, z], axis=2),
                            jnp.concatenate([z, tb], axis=2)], axis=1).astype(MXU_DTYPE)


def _compress(kc, vc, w1k, w2k, pk, w1v, w2v, pv, cos_c, sin_c):
    B, T, _ = kc.shape
    nch = T // CMP_STRIDE
    flat = CMP_STRIDE * HEAD_DIM
    posf = lambda p: jnp.broadcast_to(p.reshape(1, CMP_LEN * HEAD_DIM), (SUBLANES, CMP_LEN * HEAD_DIM)).astype(MXU_DTYPE)
    full = lambda *s: pl.BlockSpec(s, lambda b: (0,) * len(s))
    wide = 2 * NSA_KV_HEADS * CMP_HIDDEN
    kern = functools.partial(_compress_kernel, nch=nch)
    return pl.pallas_call(
        kern,
        out_shape=(jax.ShapeDtypeStruct((B, nch, LANES), MXU_DTYPE),
                   jax.ShapeDtypeStruct((B, LANES, nch), MXU_DTYPE)),
        grid=(B,),
        in_specs=[pl.BlockSpec((None, T, LANES), lambda b: (b, 0, 0)),
                  pl.BlockSpec((None, T, LANES), lambda b: (b, 0, 0)),
                  full(CMP_STRIDE, LANES, wide), full(2 * flat, CMP_HIDDEN), full(2, CMP_HIDDEN, LANES),
                  full(SUBLANES, 2 * flat),
                  full(CMP_STRIDE, LANES, wide), full(2 * flat, CMP_HIDDEN), full(2, CMP_HIDDEN, LANES),
                  full(SUBLANES, 2 * flat),
                  full(nch, LANES), full(nch, LANES)],
        out_specs=(pl.BlockSpec((None, nch, LANES), lambda b: (b, 0, 0)),
                   pl.BlockSpec((None, LANES, nch), lambda b: (b, 0, 0))),
        compiler_params=_cparams(("parallel",)),
        name="nsa_compress",
    )(kc, vc, _chunk_weights(w1k), w1k.astype(MXU_DTYPE), _pad_w2(w2k), posf(pk),
      _chunk_weights(w1v), w1v.astype(MXU_DTYPE), _pad_w2(w2v), posf(pv), cos_c, sin_c)


def _nsa_kernel(q_ref, gt_ref, kcmp_ref, vcmpt_ref, ks_ref, vst_ref, kw_ref, vwt_ref,
                ovt_ref, eselt_ref, o_ref, qa_ref, s_ref, co_ref, *, tq, wide, n_sel, nsp):
    i = pl.program_id(1)
    q0 = i * tq
    ncmp = kcmp_ref.shape[0]
    top = min(SEL_TOPK, n_sel)
    heads = range(NSA_HEADS)
    lane = _iota((tq, LANES), 1)

    for g in range(NSA_KV_HEADS):
        in_group = (lane // HEAD_DIM) == g
        for p in range(NSA_GROUP):
            blk = q_ref[:, p * LANES:(p + 1) * LANES]
            qa_ref[g, p * tq:(p + 1) * tq, :LANES] = jnp.where(in_group, blk, jnp.zeros_like(blk))

    def fresh_state():
        return [(jnp.full((1, tq), NEG, F32), jnp.zeros((HEAD_DIM + SUBLANES, tq), F32)) for _ in heads]

    def per_head(s_ts):
        return [s_ts[g][:, hl * tq:(hl + 1) * tq] for g in range(NSA_KV_HEADS) for hl in range(NSA_GROUP)]

    def flash(state, xs, bias, vt_ref, k0, w):
        new = []
        for g in range(NSA_KV_HEADS):
            vt = jnp.concatenate([vt_ref[g * HEAD_DIM:(g + 1) * HEAD_DIM, pl.ds(k0, w)],
                                  jnp.ones((SUBLANES, w), MXU_DTYPE)], axis=0)
            for hl in range(NSA_GROUP):
                m_old, acc_old = state[g * NSA_GROUP + hl]
                x = xs[g * NSA_GROUP + hl]
                if bias is not None:
                    x = x + bias
                m_new = jnp.maximum(m_old, jnp.max(x, axis=0, keepdims=True))
                p = jnp.exp(x - m_new).astype(MXU_DTYPE)
                new.append((m_new, jnp.exp(m_old - m_new) * acc_old + _dot(vt, p)))
        return new

    def gated(state, branch):
        res = []
        for h in heads:
            acc = state[h][1]
            r = N_BRANCH * h + branch
            scale = gt_ref[r:r + 1, :] * (1.0 / jnp.maximum(acc[HEAD_DIM:HEAD_DIM + 1, :], 1e-6))
            res.append(acc[:HEAD_DIM, :] * scale)
        return res

    n_win = WINDOW // tq

    def win_scores(j):
        k0 = pl.multiple_of(jnp.maximum(q0 - (n_win - j) * tq, 0), tq)
        kt = kw_ref[pl.ds(k0, tq), :]
        return [_dot_nt(kt, qa_ref[g, :, :LANES]) for g in range(NSA_KV_HEADS)], k0

    def window_branch():
        krow = _iota((tq, tq), 0)
        qcol = _iota((tq, tq), 1)
        causal_bias = jnp.where(krow <= qcol, 0.0, NEG)
        oldest_bias = jnp.where(krow > qcol, 0.0, NEG)
        w_scores, w_k0 = win_scores(0)
        win_state = fresh_state()
        for j in range(n_win + 1):
            nxt = win_scores(j + 1) if j < n_win else None
            bias = oldest_bias if j == 0 else (causal_bias if j == n_win else None)
            new_state = flash(win_state, per_head(w_scores), bias, vwt_ref, w_k0, tq)
            if j < n_win:
                live = i >= n_win - j
                new_state = [(jnp.where(live, mn, mo), jnp.where(live, an, ao))
                             for (mn, an), (mo, ao) in zip(new_state, win_state)]
                w_scores, w_k0 = nxt
            win_state = new_state
        return gated(win_state, 2)

    def compress_select(n_blk, n_row):
        kcmp = kcmp_ref[:n_row, :]
        valid_c = (_iota((n_row, tq), 0) * CMP_STRIDE + (CMP_LEN - 1)) <= (q0 + _iota((n_row, tq), 1))
        bias_c = jnp.where(valid_c, 0.0, NEG)
        keep_c = jnp.where(valid_c, 1.0, 0.0)
        n_chunks = n_blk // SUBLANES
        jrow = _iota((SUBLANES, tq), 0)
        qblk = (q0 + _iota((SUBLANES, tq), 1)) // SEL_LEN
        cmp_out = [None] * NSA_HEADS
        imp_ts = []
        for g in range(NSA_KV_HEADS):
            s_t = _dot_nt(kcmp, qa_ref[g, :, :LANES])
            vct = vcmpt_ref[g * HEAD_DIM:(g + 1) * HEAD_DIM, :n_row]
            psum = None
            for hl in range(NSA_GROUP):
                h = g * NSA_GROUP + hl
                x = s_t[:, hl * tq:(hl + 1) * tq] + bias_c
                p = jnp.exp(x - jnp.max(x, axis=0, keepdims=True)) * keep_c
                p = p * (1.0 / jnp.maximum(jnp.sum(p, axis=0, keepdims=True), 1e-6))
                psum = p if psum is None else psum + p
                cmp_out[h] = gt_ref[N_BRANCH * h:N_BRANCH * h + 1, :] * _dot(vct, p.astype(MXU_DTYPE))
            ph, plo = _split_hi_lo(psum)
            ovt = ovt_ref[:n_blk, :n_row]
            imp_ts.append(_dot(ovt, ph) + _dot(ovt, plo))
        for h, res in enumerate(window_branch()):
            co_ref[h] = cmp_out[h] + res
        for g in range(NSA_KV_HEADS):
            imp_t = imp_ts[g]
            scores = []
            for c in range(n_chunks):
                dist = qblk - (jrow + c * SUBLANES)
                forced = (dist >= 0) & (dist < N_LOCAL)
                if c == 0:
                    forced = forced | (jrow == 0)
                imp_c = imp_t[c * SUBLANES:(c + 1) * SUBLANES, :]
                scores.append(jnp.where(dist < 0, -1.0, jnp.where(forced, FORCED_SCORE, imp_c)))
            ranks = [jnp.zeros((SUBLANES, tq), F32) for _ in range(n_chunks)]
            for i2 in range(min(n_blk, n_sel)):
                c2, r2 = divmod(i2, SUBLANES)
                r = jnp.broadcast_to(scores[c2][r2:r2 + 1, :], (SUBLANES, tq))
                for c in range(n_chunks):
                    if c < c2:
                        beats = r > scores[c]
                    elif c > c2:
                        beats = r >= scores[c]
                    else:
                        beats = (r > scores[c]) | ((r == scores[c]) & (jrow > r2))
                    ranks[c] = ranks[c] + jnp.where(beats, 1.0, 0.0)
            nsel = [jnp.where((ranks[c] < top) & (scores[c] >= 0.0), 0.0, NEG) for c in range(n_chunks)]
            if n_blk < nsp:
                nsel.append(jnp.full((nsp - n_blk, tq), NEG, F32))
            if nsp < LANES:
                nsel.append(jnp.zeros((LANES - nsp, tq), F32))
            nsel_q = jnp.concatenate(nsel, axis=0).T.astype(MXU_DTYPE)
            for p in range(NSA_GROUP):
                qa_ref[g, p * tq:(p + 1) * tq, LANES:] = nsel_q

    cls_blk = 2 * SUBLANES
    n_cls = -(-nsp // cls_blk)
    cls = jnp.minimum(((q0 + tq - 1) // SEL_LEN) // cls_blk, n_cls - 1)
    for k in range(n_cls):
        n_blk = min(cls_blk * (k + 1), nsp)
        n_row = min(-(-n_blk * (SEL_LEN // CMP_STRIDE) // LANES) * LANES, ncmp)

        @pl.when(cls == k)
        def _():
            compress_select(n_blk, n_row)

    t_max = eselt_ref.shape[0] // wide - 1

    def tile_start(t):
        return pl.multiple_of(jnp.minimum(t, t_max) * wide, wide)

    def sel_scores(t, slot):
        k0 = tile_start(t)
        kaug = jnp.concatenate([ks_ref[pl.ds(k0, wide), :], eselt_ref[pl.ds(k0, wide), :]], axis=1)
        s_ts = [_dot_nt(kaug, qa_ref[g]) for g in range(NSA_KV_HEADS)]
        for h, x in enumerate(per_head(s_ts)):
            s_ref[slot, h] = x

    def sel_flash(state, t, slot, causal):
        bias = None
        if causal:
            bias = jnp.where(_iota((wide, tq), 0) - _iota((wide, tq), 1) <= q0 - t * wide, 0.0, NEG)
        xs = [s_ref[slot, h] for h in heads]
        return flash(state, xs, bias, vst_ref, tile_start(t), wide)

    def sel_body(u, state):
        sel_scores(2 * u + 1, 1)
        state = sel_flash(state, 2 * u, 0, False)
        sel_scores(2 * u + 2, 0)
        return sel_flash(state, 2 * u + 1, 1, False)

    n_tiles = (q0 + tq - 1) // wide + 1
    n_even = n_tiles + n_tiles % 2
    sel_scores(0, 0)
    sel_state = lax.fori_loop(0, n_even // 2 - 1, sel_body, fresh_state())
    sel_scores(n_even - 1, 1)
    sel_state = sel_flash(sel_state, n_even - 2, 0, True)

    sel_state = sel_flash(sel_state, n_even - 1, 1, True)
    outs = [co_ref[h] + res for h, res in enumerate(gated(sel_state, 1))]

    for hl in range(NSA_GROUP):
        pair_t = jnp.concatenate([outs[hl], outs[NSA_GROUP + hl]], axis=0)
        o_ref[:, hl * LANES:(hl + 1) * LANES] = pair_t.T


def _nsa_attention(nq, gates_t, kcmp, vcmp_t, ks, vs_t, kw, vw_t, tq, wide):
    B, T, _ = nq.shape
    ncmp = kcmp.shape[1]
    n_sel = T // SEL_LEN
    nsp = -(-n_sel // SUBLANES) * SUBLANES
    cs = jnp.arange(ncmp) * CMP_STRIDE
    ss = jnp.arange(nsp) * SEL_LEN
    ov = jnp.clip(jnp.minimum(cs[None, :] + CMP_LEN, ss[:, None] + SEL_LEN)
                  - jnp.maximum(cs[None, :], ss[:, None]), 0, None).astype(F32) / CMP_LEN
    ov = jnp.where((jnp.arange(nsp) < n_sel)[:, None] & (jnp.arange(ncmp) < ncmp - 1)[None, :], ov, 0.0)
    ovt = ov.astype(MXU_DTYPE)
    esel_t = ((jnp.arange(T) // SEL_LEN)[:, None] == jnp.arange(LANES)[None, :]).astype(MXU_DTYPE)
    kern = functools.partial(_nsa_kernel, tq=tq, wide=wide, n_sel=n_sel, nsp=nsp)
    per_b = lambda r, w: pl.BlockSpec((None, r, w), lambda b, i: (b, 0, 0))
    return pl.pallas_call(
        kern,
        out_shape=jax.ShapeDtypeStruct((B, T, NSA_WIDTH), F32),
        grid=(B, T // tq),
        in_specs=[pl.BlockSpec((None, tq, NSA_WIDTH), lambda b, i: (b, i, 0)),
                  pl.BlockSpec((None, LANES, tq), lambda b, i: (b, 0, i)),
                  per_b(ncmp, LANES), per_b(LANES, ncmp),
                  per_b(T, LANES), per_b(LANES, T), per_b(T, LANES), per_b(LANES, T),
                  pl.BlockSpec((nsp, ncmp), lambda b, i: (0, 0)),
                  pl.BlockSpec((T, LANES), lambda b, i: (0, 0))],
        out_specs=pl.BlockSpec((None, tq, NSA_WIDTH), lambda b, i: (b, i, 0)),
        scratch_shapes=[pltpu.VMEM((NSA_KV_HEADS, NSA_GROUP * tq, 2 * LANES), MXU_DTYPE),
                        pltpu.VMEM((2, NSA_HEADS, wide, tq), F32),
                        pltpu.VMEM((NSA_HEADS, HEAD_DIM, tq), F32)],
        compiler_params=_cparams(("parallel", "parallel")),
        name="nsa_attention",
    )(nq, gates_t, kcmp, vcmp_t, ks, vs_t, kw, vw_t, ovt, esel_t)


def _rms(v, g):
    return (v * lax.rsqrt(jnp.mean(v * v, axis=-1, keepdims=True) + EPS)) * g


def _ffn_kernel(x_ref, osb_ref, onsa_ref, mod_ref, gsb_ref, gnsa_ref, wo_ref, g_ref, win_ref, cw_ref,
                wdn_ref, fg_ref, o_ref, prev_ref, ubuf_ref, z_ref, *, tm, fc, final_norm):
    t = pl.program_id(1)

    @pl.when(t == 0)
    def _():
        prev_ref[...] = jnp.zeros_like(prev_ref)

    a_sb = _rms(osb_ref[...], gsb_ref[...]).astype(MXU_DTYPE)
    a_nsa = _rms(onsa_ref[...], gnsa_ref[...]).astype(MXU_DTYPE)
    attn = _dot(a_sb, wo_ref[:SB_WIDTH, :]) + _dot(a_nsa, wo_ref[SB_WIDTH:, :])
    x = x_ref[...] + mod_ref[2:3, :] * attn
    h = _rms(x, g_ref[...]) * (1.0 + mod_ref[4:5, :]) + mod_ref[3:4, :]
    hb = h.astype(MXU_DTYPE)
    def conv(u, c0, slot):
        ubuf_ref[slot, :SUBLANES, :] = prev_ref[:, c0:c0 + fc]
        ubuf_ref[slot, SUBLANES:, :] = u
        prev_ref[:, c0:c0 + fc] = u[tm - SUBLANES:, :]
        u1 = ubuf_ref[slot, SUBLANES - 1:SUBLANES - 1 + tm, :]
        u2 = ubuf_ref[slot, SUBLANES - 2:SUBLANES - 2 + tm, :]
        cw = cw_ref[:, c0:c0 + fc]
        return cw[2:3, :] * u + cw[1:2, :] * u1 + cw[0:1, :] * u2 + cw[3:4, :]

    def up(c):
        return (_dot(hb, win_ref[:, c * fc:(c + 1) * fc]),
                _dot(hb, win_ref[:, D_FF + c * fc:D_FF + (c + 1) * fc]))

    n_chunks = D_FF // fc
    ua, ub = up(0)
    for c in range(n_chunks):
        nxt = up(c + 1) if c + 1 < n_chunks else None
        ya = conv(ua, c * fc, 2 * (c % 2))
        yb = conv(ub, D_FF + c * fc, 2 * (c % 2) + 1)
        z_ref[:, c * fc:(c + 1) * fc] = ((ya * jax.nn.sigmoid(ya)) * yb).astype(MXU_DTYPE)
        if nxt is not None:
            ua, ub = nxt
    y = x + mod_ref[5:6, :] * _dot(z_ref[...], wdn_ref[...])
    if final_norm:
        y = _rms(y, fg_ref[...])
    o_ref[...] = y


def _outproj_ffn(x, o_sb, o_nsa, mod_l, g_sb, g_nsa, w_out, ln_g, w_in, conv_wb, w_down, final_g,
                 tm, fc, final_norm):
    B, T, D = x.shape
    row = lambda w: pl.BlockSpec((None, tm, w), lambda b, t: (b, t, 0))
    vec = lambda w: pl.BlockSpec((1, w), lambda b, t: (0, 0))
    const = lambda r, c: pl.BlockSpec((r, c), lambda b, t: (0, 0), pipeline_mode=pl.Buffered(1))
    kern = functools.partial(_ffn_kernel, tm=tm, fc=fc, final_norm=final_norm)
    return pl.pallas_call(
        kern,
        out_shape=jax.ShapeDtypeStruct((B, T, D), F32),
        grid=(B, T // tm),
        in_specs=[row(D), row(SB_WIDTH), row(NSA_WIDTH),
                  pl.BlockSpec((None, 6, D), lambda b, t: (b, 0, 0)),
                  vec(SB_WIDTH), vec(NSA_WIDTH), const(D, D), vec(D),
                  const(D, 2 * D_FF), const(SUBLANES, 2 * D_FF), const(D_FF, D), vec(D)],
        out_specs=row(D),
        scratch_shapes=[pltpu.VMEM((SUBLANES, 2 * D_FF), F32),
                        pltpu.VMEM((4, SUBLANES + tm, fc), F32),
                        pltpu.VMEM((tm, D_FF), MXU_DTYPE)],
        compiler_params=_cparams(("parallel", "arbitrary")),
        name="outproj_ffn",
    )(x, o_sb, o_nsa, mod_l, g_sb, g_nsa, w_out, ln_g, w_in, conv_wb, w_down, final_g)


_NSA_HEAD_ORDER = [h for p in range(NSA_GROUP) for h in (p, NSA_GROUP + p)]


def _head_cols(order):
    return jnp.asarray([h * HEAD_DIM + d for h in order for d in range(HEAD_DIM)], jnp.int32)


def _prep_w_in(w):
    nq0 = 3 * SB_WIDTH
    kv0 = nq0 + NSA_WIDTH
    gl0 = kv0 + 2 * N_BRANCH * KV_WIDTH
    nq = w[:, nq0:kv0][:, _head_cols(_NSA_HEAD_ORDER)]
    gl = jnp.pad(w[:, gl0:], ((0, 0), (0, LANES - (w.shape[1] - gl0))))
    return jnp.concatenate([w[:, :nq0], nq, w[:, kv0:gl0], gl], axis=1).astype(MXU_DTYPE)


def kernel(x, c, ln1_g, ln2_g, w_ada, b_ada, w_in, cmp_pos_k, cmp_w1_k, cmp_w2_k, cmp_pos_v, cmp_w1_v, cmp_w2_v, sb_out_g, nsa_out_g, w_out, ffn_w_in, ffn_conv_w, ffn_conv_b, ffn_w_down, final_g):
    B, T, D = x.shape
    L = w_in.shape[0]
    assert D == D_MODEL and T % 1024 == 0 and T // SEL_LEN <= LANES
    tm_proj = 1024
    tm_ffn = 512
    fc = 256
    tq = 128
    nch = T // CMP_STRIDE

    half = HEAD_DIM // 2
    inv = ROPE_THETA ** (-jnp.arange(half, dtype=F32) / half)
    inv_lanes = jnp.tile(inv, LANES // half).reshape(1, LANES)
    cos_t, sin_t = _rope_tables(inv_lanes, T, 1, 0)
    cos_c, sin_c = _rope_tables(inv_lanes, nch, CMP_STRIDE, CMP_LEN - 1)

    mod = _modulation(c, w_ada, b_ada).reshape(L, B, 6, D)
    nsa_cols = _head_cols(_NSA_HEAD_ORDER)

    for l in range(L):
        mod_l = mod[l]
        (sbq, sbk, sbv, nq, kc, vc, ks, vs, kw, vw, gates) = _projection(
            x, mod_l, ln1_g[l].reshape(1, D), _prep_w_in(w_in[l]), cos_t, sin_t, tm_proj)
        o_sb = _sb_attention(sbq, sbk, sbv, tq, 2 * tq)
        kcmp, vcmp = _compress(kc, vc, cmp_w1_k[l], cmp_w2_k[l], cmp_pos_k[l],
                               cmp_w1_v[l], cmp_w2_v[l], cmp_pos_v[l], cos_c, sin_c)
        o_nsa = _nsa_attention(nq, gates, kcmp, vcmp, ks, vs, kw, vw, 2 * tq, 2 * tq)
        w_o = jnp.concatenate([w_out[l][:SB_WIDTH], w_out[l][SB_WIDTH:][nsa_cols]], axis=0).astype(MXU_DTYPE)
        conv_wb = jnp.concatenate([ffn_conv_w[l], ffn_conv_b[l][None, :],
                                   jnp.zeros((SUBLANES - CONV_W - 1, 2 * D_FF), F32)], axis=0)
        x = _outproj_ffn(x, o_sb, o_nsa, mod_l, sb_out_g[l].reshape(1, SB_WIDTH),
                         nsa_out_g[l][nsa_cols].reshape(1, NSA_WIDTH), w_o,
                         ln2_g[l].reshape(1, D), ffn_w_in[l].astype(MXU_DTYPE), conv_wb,
                         ffn_w_down[l].astype(MXU_DTYPE), final_g.reshape(1, D), tm_ffn, fc, l == L - 1)
    return x
```
